```python
import jax, jax.numpy as jnp
from jax import lax
import numpy as np

D_MODEL = 1024
BATCH = 4
SEQ = 4096
DEPTH = 4

HEAD_DIM = 64
GLA_WIDTH = 3 * D_MODEL // 8
HGRN_WIDTH = 3 * D_MODEL // 8
MLSTM_WIDTH = D_MODEL - GLA_WIDTH - HGRN_WIDTH
GLA_HEADS = GLA_WIDTH // HEAD_DIM
GLA_DK = HEAD_DIM // 2
GLA_GATE_RANK = 16
GLA_GATE_TAU = 16.0
HGRN_HEADS = HGRN_WIDTH // HEAD_DIM
HGRN_DK = HEAD_DIM
MLSTM_HEADS = MLSTM_WIDTH // HEAD_DIM
MLSTM_CONV = 4
D_FF = 4 * D_MODEL
CHUNK = 64
EPS = 1e-6
IN_SPLITS = (GLA_HEADS * GLA_DK, GLA_HEADS * GLA_DK, GLA_WIDTH, GLA_GATE_RANK, GLA_WIDTH,
             HGRN_HEADS * HGRN_DK, HGRN_HEADS * HGRN_DK, HGRN_WIDTH, HGRN_WIDTH,
             2 * MLSTM_WIDTH, MLSTM_WIDTH, MLSTM_HEADS, MLSTM_HEADS, MLSTM_WIDTH)
D_IN = sum(IN_SPLITS)
IN_OFFSETS = tuple(int(o) for o in np.cumsum(IN_SPLITS)[:-1])

kernel_name = 'hybrid_gla_hgrn2_mlstm_adaln_trunk'


def _rms(x):
    xf = x.astype(jnp.float32)
    return xf * lax.rsqrt(jnp.mean(xf * xf, axis=-1, keepdims=True) + EPS)


def modulated_rmsnorm(x, gain, shift, scale):
    y = _rms(x) * gain.astype(jnp.float32)
    y = y * (1.0 + scale[:, None]) + shift[:, None]
    return y.astype(x.dtype)


def split_heads(t, n_heads):
    return t.reshape(t.shape[:-1] + (n_heads, -1))


def head_rmsnorm(o, gain):
    return o * lax.rsqrt(jnp.mean(o * o, axis=-1, keepdims=True) + EPS) * gain.astype(jnp.float32)


def head_layernorm(h, gain):
    mu = jnp.mean(h, axis=-1, keepdims=True)
    hc = h - mu
    var = jnp.mean(hc * hc, axis=-1, keepdims=True)
    return hc * lax.rsqrt(var + EPS) * gain.astype(jnp.float32).reshape(h.shape[-2:])


def to_chunks(t):
    b, s = t.shape[:2]
    return jnp.moveaxis(t.reshape((b, s // CHUNK, CHUNK) + t.shape[2:]), 1, 0)


def gated_linear_attention(q, k, v, log_g):
    B, S, H, K = q.shape
    V = v.shape[-1]
    causal = jnp.tril(jnp.ones((CHUNK, CHUNK), bool))[None, :, :, None, None]

    def step(state, inp):
        qc, kc, vc, gc = inp
        b = jnp.cumsum(gc, axis=1)
        decay = jnp.exp(jnp.where(causal, b[:, :, None] - b[:, None, :], -jnp.inf))
        scores = jnp.einsum('bihk,bjhk,bijhk->bhij', qc, kc, decay)
        o = (jnp.einsum('bhij,bjhv->bihv', scores, vc)
             + jnp.einsum('bihk,bhkv->bihv', qc * jnp.exp(b), state))
        b_last = b[:, -1]
        state = (jnp.exp(b_last)[..., None] * state
                 + jnp.einsum('bjhk,bjhv->bhkv', kc * jnp.exp(b_last[:, None] - b), vc))
        return state, o

    f32 = jnp.float32
    xs = (to_chunks(q.astype(f32)), to_chunks(k.astype(f32)),
          to_chunks(v.astype(f32)), to_chunks(log_g.astype(f32)))
    _, o = lax.scan(step, jnp.zeros((B, H, K, V), f32), xs)
    return jnp.moveaxis(o, 0, 1).reshape(B, S, H, V)


def mlstm_chunkwise(q, k, v, log_i, log_f):
    B, S, H, Dh = q.shape
    N = S // CHUNK

    def blk(t):
        return jnp.moveaxis(t.reshape((B, N, CHUNK) + t.shape[2:]), 3, 1)

    q, k, v, log_i, log_f = blk(q), blk(k), blk(v), blk(log_i), blk(log_f)
    b = jnp.cumsum(log_f, axis=-1)
    causal = jnp.tril(jnp.ones((CHUNK, CHUNK), bool))
    log_w = jnp.where(causal, b[..., :, None] - b[..., None, :] + log_i[..., None, :], -jnp.inf)
    m_intra = jnp.max(log_w, axis=-1)
    b_last = b[..., -1]
    w_state = b_last[..., None] - b + log_i
    m_local = jnp.max(w_state, axis=-1)
    p_state = jnp.exp(w_state - m_local[..., None])
    u = jnp.einsum('bhnc,bhncd,bhnce->bhnde', p_state, k, v)
    nu = jnp.einsum('bhnc,bhncd->bhnd', p_state, k)

    def step(carry, inp):
        c_st, n_st, m = carry
        a_n, ml_n, u_n, nu_n = inp
        m_new = jnp.maximum(a_n + m, ml_n)
        s_old = jnp.exp(a_n + m - m_new)
        s_loc = jnp.exp(ml_n - m_new)
        c_new = s_old[..., None, None] * c_st + s_loc[..., None, None] * u_n
        n_new = s_old[..., None] * n_st + s_loc[..., None] * nu_n
        return (c_new, n_new, m_new), (c_st, n_st, m)

    f32 = jnp.float32
    init = (jnp.zeros((B, H, Dh, Dh), f32), jnp.zeros((B, H, Dh), f32), jnp.zeros((B, H), f32))
    xs = (jnp.moveaxis(b_last, 2, 0), jnp.moveaxis(m_local, 2, 0),
          jnp.moveaxis(u, 2, 0), jnp.moveaxis(nu, 2, 0))
    _, (c_prev, n_prev, m_prev) = lax.scan(step, init, xs)
    c_prev = jnp.moveaxis(c_prev, 0, 2)
    n_prev = jnp.moveaxis(n_prev, 0, 2)
    m_prev = jnp.moveaxis(m_prev, 0, 2)
    m_inter = b + m_prev[..., None]
    m_t = jnp.maximum(m_inter, m_intra)
    p = jnp.exp(log_w - m_t[..., None]) * jnp.einsum('bhnid,bhnjd->bhnij', q, k)
    s_inter = jnp.exp(m_inter - m_t)
    num = (jnp.einsum('bhnij,bhnje->bhnie', p, v)
           + s_inter[..., None] * jnp.einsum('bhnid,bhnde->bhnie', q, c_prev))
    den = jnp.sum(p, axis=-1) + s_inter * jnp.einsum('bhnid,bhnd->bhni', q, n_prev)
    h = num / jnp.maximum(jnp.abs(den), jnp.exp(-m_t))[..., None]
    return h.transpose(0, 2, 3, 1, 4).reshape(B, S, H, Dh)


def causal_depthwise_conv(x, w):
    return lax.conv_general_dilated(
        x, w[:, None, :].astype(x.dtype), window_strides=(1,), padding=[(MLSTM_CONV - 1, 0)],
        dimension_numbers=('NWC', 'WIO', 'NWC'), feature_group_count=x.shape[-1])


def gla_group(q, k, v, g_low, g_out, w_gate, b_gate, norm_gain):
    f32 = jnp.float32
    B, S, _ = q.shape
    q = split_heads(q.astype(f32), GLA_HEADS) * GLA_DK ** -0.5
    k = split_heads(k.astype(f32), GLA_HEADS)
    v = split_heads(v, GLA_HEADS)
    gate_logits = (g_low @ w_gate).astype(f32) + b_gate.astype(f32)
    log_g = split_heads(jax.nn.log_sigmoid(gate_logits) / GLA_GATE_TAU, GLA_HEADS)
    o = gated_linear_attention(q, k, v, log_g)
    o = head_rmsnorm(o, norm_gain) * jax.nn.silu(split_heads(g_out.astype(f32), GLA_HEADS))
    return o.reshape(B, S, GLA_WIDTH)


def hgrn2_group(q, f, i, g_out, lower_bound, norm_gain):
    f32 = jnp.float32
    B, S, _ = q.shape
    lb = lower_bound.reshape(HGRN_HEADS, HGRN_DK)
    f = split_heads(f.astype(f32), HGRN_HEADS)
    log_f = jnp.logaddexp(jnp.log(lb), jnp.log1p(-lb) + jax.nn.log_sigmoid(f))
    k = (1.0 - lb) * jax.nn.sigmoid(-f)
    q = jax.nn.silu(split_heads(q.astype(f32), HGRN_HEADS))
    o = gated_linear_attention(q, k, split_heads(i, HGRN_HEADS), log_f)
    o = head_rmsnorm(o, norm_gain) * jax.nn.sigmoid(split_heads(g_out.astype(f32), HGRN_HEADS))
    return o.reshape(B, S, HGRN_WIDTH)


def mlstm_group(qk_pre, v, i_pre, f_pre, g_out, conv_w, gate_b, norm_gain):
    f32 = jnp.float32
    B, S, _ = v.shape
    qk = jax.nn.silu(causal_depthwise_conv(qk_pre, conv_w).astype(f32))
    q, k = jnp.split(qk, 2, axis=-1)
    q = split_heads(q, MLSTM_HEADS)
    k = split_heads(k, MLSTM_HEADS) * HEAD_DIM ** -0.5
    gate_b = gate_b.astype(f32)
    log_i = i_pre.astype(f32) + gate_b[:MLSTM_HEADS]
    log_f = jax.nn.log_sigmoid(f_pre.astype(f32) + gate_b[MLSTM_HEADS:])
    h = mlstm_chunkwise(q, k, split_heads(v.astype(f32), MLSTM_HEADS), log_i, log_f)
    h = head_layernorm(h, norm_gain) * jax.nn.sigmoid(split_heads(g_out.astype(f32), MLSTM_HEADS))
    return h.reshape(B, S, MLSTM_WIDTH)


def setup_inputs(seed: int = 0) -> dict:
    key = jax.random.key(seed)
    ks = jax.random.split(key, 22)
    f32 = jnp.float32

    def nrm(k, shape, std):
        return jax.random.normal(k, shape, f32) * std

    forget_bias = jnp.linspace(3.0, 6.0, MLSTM_HEADS, dtype=f32)[None, :] + nrm(ks[15], (DEPTH, MLSTM_HEADS), 0.1)
    return {
        'x': nrm(ks[0], (BATCH, SEQ, D_MODEL), 1.0),
        'c': nrm(ks[1], (BATCH, D_MODEL), 1.0),
        'w_ada': nrm(ks[2], (DEPTH, D_MODEL, 6 * D_MODEL), 0.5 * D_MODEL ** -0.5),
        'b_ada': nrm(ks[3], (DEPTH, 6 * D_MODEL), 0.02),
        'norm_mix': 1.0 + nrm(ks[4], (DEPTH, D_MODEL), 0.02),
        'norm_mlp': 1.0 + nrm(ks[5], (DEPTH, D_MODEL), 0.02),
        'w_in': nrm(ks[6], (DEPTH, D_MODEL, D_IN), D_MODEL ** -0.5),
        'gla_w_gate': nrm(ks[7], (DEPTH, GLA_GATE_RANK, GLA_HEADS * GLA_DK), GLA_GATE_RANK ** -0.5),
        'gla_b_gate': nrm(ks[8], (DEPTH, GLA_HEADS * GLA_DK), 0.1),
        'gla_norm': 1.0 + nrm(ks[9], (DEPTH, HEAD_DIM), 0.02),
        'hgrn_lb_logits': nrm(ks[10], (DEPTH, HGRN_HEADS * HGRN_DK), 0.1),
        'hgrn_norm': 1.0 + nrm(ks[11], (DEPTH, HEAD_DIM), 0.02),
        'mlstm_conv': nrm(ks[12], (DEPTH, MLSTM_CONV, 2 * MLSTM_WIDTH), MLSTM_CONV ** -0.5),
        'mlstm_gate_b': jnp.concatenate([nrm(ks[14], (DEPTH, MLSTM_HEADS), 0.1), forget_bias], axis=-1),
        'mlstm_norm': 1.0 + nrm(ks[16], (DEPTH, MLSTM_WIDTH), 0.02),
        'w_out': nrm(ks[17], (DEPTH, D_MODEL, D_MODEL), D_MODEL ** -0.5),
        'w_ff1': nrm(ks[18], (DEPTH, D_MODEL, D_FF), D_MODEL ** -0.5),
        'w_ff2': nrm(ks[19], (DEPTH, D_FF, D_MODEL), D_FF ** -0.5),
        'final_norm': 1.0 + nrm(ks[20], (D_MODEL,), 0.02),
    }


def reference(x, c, w_ada, b_ada, norm_mix, norm_mlp, w_in, gla_w_gate, gla_b_gate, gla_norm,
              hgrn_lb_logits, hgrn_norm, mlstm_conv, mlstm_gate_b, mlstm_norm, w_out, w_ff1, w_ff2,
              final_norm):
    f32 = jnp.float32
    cond = jax.nn.silu(c.astype(f32))
    lb_cum = jnp.cumsum(jax.nn.softmax(hgrn_lb_logits.astype(f32), axis=0), axis=0)
    lower_bounds = lb_cum - lb_cum[0]
    for l in range(DEPTH):
        mod = cond @ w_ada[l].astype(f32) + b_ada[l].astype(f32)
        shift1, scale1, gate1, shift2, scale2, gate2 = jnp.split(mod, 6, axis=-1)
        h = modulated_rmsnorm(x, norm_mix[l], shift1, scale1)
        z = h @ w_in[l]
        (gq, gk, gv, glow, gout, hq, hf, hi, hout,
         mqk, mv, mi, mf, mout) = jnp.split(z, IN_OFFSETS, axis=-1)
        mixed = jnp.concatenate([
            gla_group(gq, gk, gv, glow, gout, gla_w_gate[l], gla_b_gate[l], gla_norm[l]),
            hgrn2_group(hq, hf, hi, hout, lower_bounds[l], hgrn_norm[l]),
            mlstm_group(mqk, mv, mi, mf, mout, mlstm_conv[l], mlstm_gate_b[l], mlstm_norm[l]),
        ], axis=-1).astype(x.dtype)
        x = x + gate1[:, None].astype(x.dtype) * (mixed @ w_out[l])
        h = modulated_rmsnorm(x, norm_mlp[l], shift2, scale2)
        ff = jnp.square(jax.nn.relu(h @ w_ff1[l])) @ w_ff2[l]
        x = x + gate2[:, None].astype(x.dtype) * ff
    return (_rms(x) * final_norm.astype(f32)).astype(x.dtype)
```

```python
import functools

import jax
import jax.numpy as jnp
from jax import lax
from jax.experimental import pallas as pl
from jax.experimental.pallas import tpu as pltpu

F32 = jnp.float32
BF16 = jnp.bfloat16

HEAD_DIM = 64
GLA_HEADS = 6
GLA_DK = 32
GLA_RANK = 16
GLA_TAU = 16.0
HGRN_HEADS = 6
HGRN_DK = 64
MLSTM_HEADS = 4
CONV_WIDTH = 4
EPS = 1e-6
CHUNK = 64
SUB = 16
LANE = 128
NEG = -1e30

GLA_QK = GLA_HEADS * GLA_DK
GLA_W = GLA_HEADS * HEAD_DIM
HGRN_W = HGRN_HEADS * HEAD_DIM
MLSTM_W = MLSTM_HEADS * HEAD_DIM

_REF_SEGMENTS = (("gq", GLA_QK), ("gk", GLA_QK), ("gv", GLA_W), ("glow", GLA_RANK), ("gout", GLA_W),
                 ("hq", HGRN_W), ("hf", HGRN_W), ("hi", HGRN_W), ("hout", HGRN_W),
                 ("mqk", 2 * MLSTM_W), ("mv", MLSTM_W), ("mi", MLSTM_HEADS), ("mf", MLSTM_HEADS),
                 ("mout", MLSTM_W))
_MY_ORDER = ("gq", "gk", "gv", "gout", "hq", "hf", "hi", "hout", "mqk", "mv", "mout", "mf", "mi", "glow")


def _layout():
    ref_off, off = {}, 0
    for name, width in _REF_SEGMENTS:
        ref_off[name] = (off, width)
        off += width
    my_off, off = {}, 0
    for name in _MY_ORDER:
        my_off[name] = off
        off += ref_off[name][1]
    padded = -(-off // LANE) * LANE
    return ref_off, my_off, off, padded


_REF_OFF, OFF, _NZ_USED, NZ = _layout()
SMALL = OFF["mf"]
assert SMALL % LANE == 0 and OFF["mi"] == SMALL + MLSTM_HEADS and OFF["glow"] == SMALL + 2 * MLSTM_HEADS


def _permute_in_proj(w_in):
    parts = [w_in[..., _REF_OFF[n][0]:_REF_OFF[n][0] + _REF_OFF[n][1]] for n in _MY_ORDER]
    parts.append(jnp.zeros(w_in.shape[:-1] + (NZ - _NZ_USED,), w_in.dtype))
    return jnp.concatenate(parts, axis=-1).astype(BF16)


def _silu(x):
    return x * jax.nn.sigmoid(x)


def _log_sigmoid(x):
    return jnp.minimum(x, 0.0) - jnp.log1p(jnp.exp(-jnp.abs(x)))


def _dot(a, b):
    return jnp.dot(a.astype(BF16), b.astype(BF16), preferred_element_type=F32)


def _dot_nt(a, b):
    return lax.dot_general(a.astype(BF16), b.astype(BF16), (((1,), (1,)), ((), ())),
                           preferred_element_type=F32)


def _dot_tn(a, b):
    return lax.dot_general(a.astype(BF16), b.astype(BF16), (((0,), (0,)), ((), ())),
                           preferred_element_type=F32)


def _cumsum_rows(g, tri):
    hi = g.astype(BF16)
    lo = (g - hi.astype(F32)).astype(BF16)
    return (jnp.dot(tri, hi, preferred_element_type=F32) + jnp.dot(tri, lo, preferred_element_type=F32))


def _adaln_kernel(c_ref, w_ref, b_ref, o_ref):
    cond = _silu(c_ref[...])
    o_ref[...] = jnp.dot(cond, w_ref[...], preferred_element_type=F32,
                         precision=lax.Precision.HIGHEST) + b_ref[...]


def _adaln(c, w_ada, b_ada, *, tn=2048):
    depth, d, n = w_ada.shape
    bsz = c.shape[0]
    return pl.pallas_call(
        _adaln_kernel,
        grid=(depth, n // tn),
        in_specs=[pl.BlockSpec((bsz, d), lambda l, j: (0, 0)),
                  pl.BlockSpec((None, d, tn), lambda l, j: (l, 0, j)),
                  pl.BlockSpec((None, 1, tn), lambda l, j: (l, 0, j))],
        out_specs=pl.BlockSpec((None, bsz, tn), lambda l, j: (l, 0, j)),
        out_shape=jax.ShapeDtypeStruct((depth, bsz, n), F32),
        name="adaln_modulation",
    )(c, w_ada, b_ada.reshape(depth, 1, n))


def _lower_bound_kernel(logit_ref, o_ref):
    x = logit_ref[...]
    e = jnp.exp(x - jnp.max(x, axis=0, keepdims=True))
    p = e / jnp.sum(e, axis=0, keepdims=True)
    depth = x.shape[0]
    acc = p[0:1]
    rows = [acc]
    for l in range(1, depth):
        acc = acc + p[l:l + 1]
        rows.append(acc)
    cum = jnp.concatenate(rows, axis=0)
    o_ref[...] = cum - cum[0:1]


def _lower_bounds(lb_logits):
    return pl.pallas_call(
        _lower_bound_kernel,
        out_shape=jax.ShapeDtypeStruct(lb_logits.shape, F32),
        name="hgrn_lower_bounds",
    )(lb_logits)


def _rms(x):
    return x * lax.rsqrt(jnp.mean(x * x, axis=-1, keepdims=True) + EPS)


def _in_proj_kernel(x_ref, shift_ref, scale_ref, gain_ref, w_ref, z_ref):
    y = _rms(x_ref[...]) * gain_ref[...]
    y = y * (1.0 + scale_ref[...]) + shift_ref[...]
    z_ref[...] = jnp.dot(y.astype(BF16), w_ref[...], preferred_element_type=F32)


def _in_proj(x, shift, scale, gain, w, *, tm=256):
    bsz, s, d = x.shape
    nz = w.shape[-1]
    return pl.pallas_call(
        _in_proj_kernel,
        grid=(bsz, s // tm),
        in_specs=[pl.BlockSpec((None, tm, d), lambda b, i: (b, i, 0)),
                  pl.BlockSpec((None, 1, d), lambda b, i: (b, 0, 0)),
                  pl.BlockSpec((None, 1, d), lambda b, i: (b, 0, 0)),
                  pl.BlockSpec((1, d), lambda b, i: (0, 0)),
                  pl.BlockSpec((d, nz), lambda b, i: (0, 0))],
        out_specs=pl.BlockSpec((None, tm, nz), lambda b, i: (b, i, 0)),
        out_shape=jax.ShapeDtypeStruct((bsz, s, nz), F32),
        compiler_params=pltpu.CompilerParams(dimension_semantics=("arbitrary", "arbitrary")),
        name="norm_in_proj",
    )(x, shift, scale, gain, w)


def _decayed_attention(q, k, g, v_ref_slices, state_ref, tri, causal, n_heads, dk):
    b = _cumsum_rows(g, tri)
    b_last = b[CHUNK - 1:CHUNK, :]
    n_groups = CHUNK // SUB
    refs = [b[i * SUB + SUB // 2:i * SUB + SUB // 2 + 1, :] for i in range(n_groups)]
    ref_rows = jnp.concatenate([jnp.broadcast_to(r, (SUB, r.shape[-1])) for r in refs], axis=0)
    q_in = q * jnp.exp(b - ref_rows)
    q_state = q * jnp.exp(b)
    k_state = k * jnp.exp(b_last - b)
    row = lax.broadcasted_iota(jnp.int32, (CHUNK, 1), 0)
    k_in = [k * jnp.exp(jnp.where(row < (i + 1) * SUB, refs[i] - b, NEG)) for i in range(n_groups)]
    decay_last = jnp.exp(b_last)
    outs = []
    for h in range(n_heads):
        hs = slice(h * dk, (h + 1) * dk)
        scores = jnp.concatenate(
            [_dot_nt(q_in[i * SUB:(i + 1) * SUB, hs], k_in[i][:, hs]) for i in range(n_groups)], axis=0)
        scores = jnp.where(causal, scores, 0.0)
        v = v_ref_slices(h)
        st = state_ref[h]
        outs.append(_dot(scores, v) + _dot_nt(q_state[:, hs], st))
        state_ref[h] = st * decay_last[:, hs] + _dot_tn(v, k_state[:, hs])
    return outs


def _mixer_kernel(z_ref, wgate_ref, bgate_ref, gnorm_ref, lb_ref, hnorm_ref, conv_ref, gateb_ref, mnorm_ref,
                  o_ref, gla_state, hgrn_state, mlstm_state, mlstm_m, conv_carry, *, n_chunks):
    @pl.when(pl.program_id(1) == 0)
    def _():
        gla_state[...] = jnp.zeros_like(gla_state)
        hgrn_state[...] = jnp.zeros_like(hgrn_state)
        mlstm_state[...] = jnp.zeros_like(mlstm_state)
        mlstm_m[...] = jnp.zeros_like(mlstm_m)
        conv_carry[...] = jnp.zeros_like(conv_carry)

    r_i = lax.broadcasted_iota(jnp.int32, (CHUNK, CHUNK), 0)
    c_i = lax.broadcasted_iota(jnp.int32, (CHUNK, CHUNK), 1)
    causal = r_i >= c_i
    tri = causal.astype(BF16)

    def chunk_step(ci, carry):
        rows = pl.ds(pl.multiple_of(ci * CHUNK, CHUNK), CHUNK)

        def seg(name, lo, width):
            return z_ref[rows, OFF[name] + lo:OFF[name] + lo + width]

        small = z_ref[rows, SMALL:SMALL + LANE]
        glow = small[:, 2 * MLSTM_HEADS:2 * MLSTM_HEADS + GLA_RANK]
        logits = _dot(glow, wgate_ref[...]) + bgate_ref[...]
        g_gla = _log_sigmoid(logits) * (1.0 / GLA_TAU)
        q_gla = seg("gq", 0, GLA_QK) * (GLA_DK ** -0.5)
        k_gla = seg("gk", 0, GLA_QK)
        o_gla = _decayed_attention(q_gla, k_gla, g_gla, lambda h: seg("gv", h * HEAD_DIM, HEAD_DIM),
                                   gla_state, tri, causal, GLA_HEADS, GLA_DK)
        for h in range(GLA_HEADS):
            o = o_gla[h]
            o = o * lax.rsqrt(jnp.mean(o * o, axis=-1, keepdims=True) + EPS) * gnorm_ref[...]
            o = o * _silu(seg("gout", h * HEAD_DIM, HEAD_DIM))
            o_ref[rows, h * HEAD_DIM:(h + 1) * HEAD_DIM] = o.astype(o_ref.dtype)

        lb = lb_ref[...]
        f = seg("hf", 0, HGRN_W)
        log_lb = jnp.log(lb)
        log_rest = jnp.log1p(-lb) + _log_sigmoid(f)
        g_hgrn = jnp.maximum(log_lb, log_rest) + jnp.log1p(jnp.exp(-jnp.abs(log_lb - log_rest)))
        k_hgrn = (1.0 - lb) * jax.nn.sigmoid(-f)
        q_hgrn = _silu(seg("hq", 0, HGRN_W))
        o_hgrn = _decayed_attention(q_hgrn, k_hgrn, g_hgrn, lambda h: seg("hi", h * HEAD_DIM, HEAD_DIM),
                                    hgrn_state, tri, causal, HGRN_HEADS, HGRN_DK)
        for h in range(HGRN_HEADS):
            o = o_hgrn[h]
            o = o * lax.rsqrt(jnp.mean(o * o, axis=-1, keepdims=True) + EPS) * hnorm_ref[...]
            o = o * jax.nn.sigmoid(seg("hout", h * HEAD_DIM, HEAD_DIM))
            o_ref[rows, GLA_W + h * HEAD_DIM:GLA_W + (h + 1) * HEAD_DIM] = o.astype(o_ref.dtype)

        x_qk = seg("mqk", 0, 2 * MLSTM_W)
        hist = jnp.concatenate([conv_carry[...], x_qk], axis=0)
        conv_carry[...] = x_qk[CHUNK - 8:CHUNK, :]
        conv = conv_ref[0:1, :] * hist[8 - (CONV_WIDTH - 1):8 - (CONV_WIDTH - 1) + CHUNK, :]
        for t in range(1, CONV_WIDTH):
            lo = 8 - (CONV_WIDTH - 1) + t
            conv = conv + conv_ref[t:t + 1, :] * hist[lo:lo + CHUNK, :]
        qk = _silu(conv)
        q_m = qk[:, :MLSTM_W]
        k_m = qk[:, MLSTM_W:] * (HEAD_DIM ** -0.5)

        lane = lax.broadcasted_iota(jnp.int32, (1, LANE), 1)
        pre = small + gateb_ref[...]
        gates = jnp.where(lane < MLSTM_HEADS, _log_sigmoid(pre),
                          jnp.where(lane < 2 * MLSTM_HEADS, pre, 0.0))
        cum = _cumsum_rows(gates, tri)
        col_form = jnp.where(lane < MLSTM_HEADS, cum, gates)
        row_form = col_form.T
        ones_col = (lax.broadcasted_iota(jnp.int32, (CHUNK, HEAD_DIM), 1) == 0).astype(F32)
        for h in range(MLSTM_HEADS):
            hs = slice(h * HEAD_DIM, (h + 1) * HEAD_DIM)
            b_c = col_form[:, h:h + 1]
            li_c = col_form[:, MLSTM_HEADS + h:MLSTM_HEADS + h + 1]
            b_r = row_form[h:h + 1, :]
            li_r = row_form[MLSTM_HEADS + h:MLSTM_HEADS + h + 1, :]
            b_end = b_c[CHUNK - 1:CHUNK, :]
            m_prev = mlstm_m[0:1, h:h + 1]
            log_w = jnp.where(causal, b_c - b_r + li_r, NEG)
            m_intra = jnp.max(log_w, axis=-1, keepdims=True)
            m_inter = b_c + m_prev
            m_t = jnp.maximum(m_inter, m_intra)
            p = jnp.exp(log_w - m_t) * _dot_nt(q_m[:, hs], k_m[:, hs])
            s_inter = jnp.exp(m_inter - m_t)
            v_ext = jnp.concatenate([seg("mv", h * HEAD_DIM, HEAD_DIM), ones_col], axis=1)
            st = mlstm_state[h]
            res = _dot(p, v_ext) + s_inter * _dot(q_m[:, hs], st)
            num = res[:, :HEAD_DIM]
            den = res[:, HEAD_DIM:HEAD_DIM + 1]
            hh = num / jnp.maximum(jnp.abs(den), jnp.exp(-m_t))
            w_state = b_end - b_c + li_c
            m_loc = jnp.max(w_state, axis=0, keepdims=True)
            p_state = jnp.exp(w_state - m_loc)
            m_new = jnp.maximum(b_end + m_prev, m_loc)
            s_old = jnp.exp(b_end + m_prev - m_new)
            s_loc = jnp.exp(m_loc - m_new)
            mlstm_state[h] = s_old * st + s_loc * _dot_tn(k_m[:, hs] * p_state, v_ext)
            mlstm_m[0:1, h:h + 1] = m_new
            mu = jnp.mean(hh, axis=-1, keepdims=True)
            hc = hh - mu
            var = jnp.mean(hc * hc, axis=-1, keepdims=True)
            y = hc * lax.rsqrt(var + EPS) * mnorm_ref[:, hs]
            y = y * jax.nn.sigmoid(seg("mout", h * HEAD_DIM, HEAD_DIM))
            base = GLA_W + HGRN_W
            o_ref[rows, base + h * HEAD_DIM:base + (h + 1) * HEAD_DIM] = y.astype(o_ref.dtype)
        return carry

    lax.fori_loop(0, n_chunks, chunk_step, 0)


def _mixers(z, wgate, bgate, gnorm, lb, hnorm, conv_w, gate_b, mnorm, *, ts=256):
    bsz, s, nz = z.shape
    d_out = GLA_W + HGRN_W + MLSTM_W
    const = lambda b, i: (0, 0)
    return pl.pallas_call(
        functools.partial(_mixer_kernel, n_chunks=ts // CHUNK),
        grid=(bsz, s // ts),
        in_specs=[pl.BlockSpec((None, ts, nz), lambda b, i: (b, i, 0)),
                  pl.BlockSpec(wgate.shape, const), pl.BlockSpec(bgate.shape, const),
                  pl.BlockSpec(gnorm.shape, const), pl.BlockSpec(lb.shape, const),
                  pl.BlockSpec(hnorm.shape, const), pl.BlockSpec(conv_w.shape, const),
                  pl.BlockSpec(gate_b.shape, const), pl.BlockSpec(mnorm.shape, const)],
        out_specs=pl.BlockSpec((None, ts, d_out), lambda b, i: (b, i, 0)),
        out_shape=jax.ShapeDtypeStruct((bsz, s, d_out), BF16),
        scratch_shapes=[pltpu.VMEM((GLA_HEADS, HEAD_DIM, GLA_DK), F32),
                        pltpu.VMEM((HGRN_HEADS, HEAD_DIM, HGRN_DK), F32),
                        pltpu.VMEM((MLSTM_HEADS, HEAD_DIM, 2 * HEAD_DIM), F32),
                        pltpu.VMEM((8, LANE), F32),
                        pltpu.VMEM((8, 2 * MLSTM_W), F32)],
        compiler_params=pltpu.CompilerParams(dimension_semantics=("arbitrary", "arbitrary")),
        name="token_mixers",
    )(z, wgate, bgate, gnorm, lb, hnorm, conv_w, gate_b, mnorm)


def _out_mlp_kernel(mixed_ref, x_ref, gate1_ref, shift_ref, scale_ref, gate2_ref, gain_ref,
                    wout_ref, w1_ref, w2_ref, fnorm_ref, o_ref, *, final):
    x = x_ref[...] + gate1_ref[...] * jnp.dot(mixed_ref[...], wout_ref[...], preferred_element_type=F32)
    y = _rms(x) * gain_ref[...]
    y = y * (1.0 + scale_ref[...]) + shift_ref[...]
    u = jnp.dot(y.astype(BF16), w1_ref[...], preferred_element_type=F32)
    u = jnp.square(jnp.maximum(u, 0.0))
    x = x + gate2_ref[...] * jnp.dot(u.astype(BF16), w2_ref[...], preferred_element_type=F32)
    if final:
        x = _rms(x) * fnorm_ref[...]
    o_ref[...] = x


def _out_mlp(mixed, x, gate1, shift, scale, gate2, gain, w_out, w1, w2, fnorm, *, final, tm=256):
    bsz, s, d = x.shape
    dff = w1.shape[-1]
    tok = pl.BlockSpec((None, tm, d), lambda b, i: (b, i, 0))
    mod = pl.BlockSpec((None, 1, d), lambda b, i: (b, 0, 0))
    const = lambda b, i: (0, 0)
    return pl.pallas_call(
        functools.partial(_out_mlp_kernel, final=final),
        grid=(bsz, s // tm),
        in_specs=[tok, tok, mod, mod, mod, mod, pl.BlockSpec((1, d), const),
                  pl.BlockSpec((d, d), const), pl.BlockSpec((d, dff), const), pl.BlockSpec((dff, d), const),
                  pl.BlockSpec((1, d), const)],
        out_specs=tok,
        out_shape=jax.ShapeDtypeStruct((bsz, s, d), F32),
        compiler_params=pltpu.CompilerParams(dimension_semantics=("arbitrary", "arbitrary"),
                                             vmem_limit_bytes=56 * 1024 * 1024),
        name="out_proj_mlp",
    )(mixed, x, gate1, shift, scale, gate2, gain, w_out, w1, w2, fnorm)


def kernel(x, c, w_ada, b_ada, norm_mix, norm_mlp, w_in, gla_w_gate, gla_b_gate, gla_norm, hgrn_lb_logits,
           hgrn_norm, mlstm_conv, mlstm_gate_b, mlstm_norm, w_out, w_ff1, w_ff2, final_norm):
    depth, d, _ = w_ada.shape
    bsz = x.shape[0]
    mod = _adaln(c.astype(F32), w_ada.astype(F32), b_ada.astype(F32))
    mod = mod.reshape(depth, bsz, 6, 1, d)
    lower = _lower_bounds(hgrn_lb_logits.astype(F32))
    w_in_p = _permute_in_proj(w_in)
    gate_b = jnp.concatenate([mlstm_gate_b[:, MLSTM_HEADS:], mlstm_gate_b[:, :MLSTM_HEADS],
                              jnp.zeros((depth, LANE - 2 * MLSTM_HEADS), F32)], axis=-1)
    w_out_b, w1_b, w2_b = w_out.astype(BF16), w_ff1.astype(BF16), w_ff2.astype(BF16)
    wgate_b = gla_w_gate.astype(BF16)
    fnorm = final_norm.reshape(1, d).astype(F32)
    for l in range(depth):
        shift1, scale1, gate1, shift2, scale2, gate2 = (mod[l, :, j] for j in range(6))
        z = _in_proj(x, shift1, scale1, norm_mix[l].reshape(1, d), w_in_p[l])
        mixed = _mixers(z, wgate_b[l], gla_b_gate[l].reshape(1, -1), gla_norm[l].reshape(1, -1),
                        lower[l].reshape(1, -1), hgrn_norm[l].reshape(1, -1), mlstm_conv[l],
                        gate_b[l].reshape(1, -1), mlstm_norm[l].reshape(1, -1))
        x = _out_mlp(mixed, x, gate1, shift2, scale2, gate2, norm_mlp[l].reshape(1, d),
                     w_out_b[l], w1_b[l], w2_b[l], fnorm, final=(l == depth - 1))
    return x
```

```python
import functools

import numpy as np
import jax
import jax.numpy as jnp
from jax import lax
from jax.experimental import pallas as pl
from jax.experimental.pallas import tpu as pltpu

F32 = jnp.float32
BF16 = jnp.bfloat16

HEAD_DIM = 64
GLA_HEADS = 6
GLA_DK = 32
GLA_RANK = 16
GLA_TAU = 16.0
HGRN_HEADS = 6
HGRN_DK = 64
MLSTM_HEADS = 4
CONV_WIDTH = 4
EPS = 1e-6
CHUNK = 64
PIECE = 16
LANE = 128
NEG = -1e30
MAX_EXP = 80.0

GLA_QK = GLA_HEADS * GLA_DK
GLA_W = GLA_HEADS * HEAD_DIM
HGRN_W = HGRN_HEADS * HEAD_DIM
MLSTM_W = MLSTM_HEADS * HEAD_DIM
GLA_QK_PAD = 2 * LANE

_REF_SEGMENTS = (("gq", GLA_QK), ("gk", GLA_QK), ("gv", GLA_W), ("glow", GLA_RANK), ("gout", GLA_W),
                 ("hq", HGRN_W), ("hf", HGRN_W), ("hi", HGRN_W), ("hout", HGRN_W),
                 ("mqk", 2 * MLSTM_W), ("mv", MLSTM_W), ("mi", MLSTM_HEADS), ("mf", MLSTM_HEADS),
                 ("mout", MLSTM_W))
_MY_ORDER = (("gq", 0), ("pad", GLA_QK_PAD - GLA_QK), ("gk", 0), ("pad", GLA_QK_PAD - GLA_QK),
             ("gv", 0), ("gout", 0), ("hq", 0), ("hf", 0), ("hi", 0), ("hout", 0),
             ("mqk", 0), ("mv", 0), ("mout", 0), ("mf", 0), ("mi", 0), ("glow", 0))


def _layout():
    ref_off, off = {}, 0
    for name, width in _REF_SEGMENTS:
        ref_off[name] = (off, width)
        off += width
    my_off, off = {}, 0
    for name, pad in _MY_ORDER:
        if name == "pad":
            off += pad
        else:
            my_off[name] = off
            off += ref_off[name][1]
    padded = -(-off // LANE) * LANE
    return ref_off, my_off, off, padded


_REF_OFF, OFF, _NZ_USED, NZ = _layout()
SMALL = OFF["mf"]
GLOW_LANE = OFF["glow"] - SMALL
assert SMALL % LANE == 0 and OFF["mi"] == SMALL + MLSTM_HEADS and GLOW_LANE == 2 * MLSTM_HEADS
assert all(OFF[n] % LANE == 0 for n in ("gq", "gk", "gv", "gout", "hq", "hf", "hi", "hout", "mqk", "mv", "mout"))


def _permute_in_proj(w_in):
    parts, off = [], 0
    for name, pad in _MY_ORDER:
        if name == "pad":
            parts.append(jnp.zeros(w_in.shape[:-1] + (pad,), w_in.dtype))
            off += pad
        else:
            lo, width = _REF_OFF[name]
            parts.append(w_in[..., lo:lo + width])
            off += width
    parts.append(jnp.zeros(w_in.shape[:-1] + (NZ - off,), w_in.dtype))
    return jnp.concatenate(parts, axis=-1).astype(BF16)


def _silu(x):
    return x * jax.nn.sigmoid(x)


def _log_sigmoid(x):
    return jnp.minimum(x, 0.0) - jnp.log1p(jnp.exp(-jnp.abs(x)))


def _dot(a, b):
    return jnp.dot(a.astype(BF16), b.astype(BF16), preferred_element_type=F32)


def _dot_nt(a, b):
    return lax.dot_general(a.astype(BF16), b.astype(BF16), (((1,), (1,)), ((), ())),
                           preferred_element_type=F32)


def _cumsum_rows(g, tri):
    hi = g.astype(BF16)
    lo = (g - hi.astype(F32)).astype(BF16)
    return (jnp.dot(tri, hi, preferred_element_type=F32) + jnp.dot(tri, lo, preferred_element_type=F32))


def _adaln_kernel(c_ref, w_ref, b_ref, o_ref):
    cond = _silu(c_ref[...])
    o_ref[...] = jnp.dot(cond, w_ref[...], preferred_element_type=F32,
                         precision=lax.Precision.HIGHEST) + b_ref[...]


def _adaln(c, w_ada, b_ada, *, tn=2048):
    depth, d, n = w_ada.shape
    bsz = c.shape[0]
    return pl.pallas_call(
        _adaln_kernel,
        grid=(depth, n // tn),
        in_specs=[pl.BlockSpec((bsz, d), lambda l, j: (0, 0)),
                  pl.BlockSpec((None, d, tn), lambda l, j: (l, 0, j)),
                  pl.BlockSpec((None, 1, tn), lambda l, j: (l, 0, j))],
        out_specs=pl.BlockSpec((None, bsz, tn), lambda l, j: (l, 0, j)),
        out_shape=jax.ShapeDtypeStruct((depth, bsz, n), F32),
        name="adaln_modulation",
    )(c, w_ada, b_ada.reshape(depth, 1, n))


def _lower_bound_kernel(logit_ref, o_ref):
    x = logit_ref[...]
    e = jnp.exp(x - jnp.max(x, axis=0, keepdims=True))
    p = e / jnp.sum(e, axis=0, keepdims=True)
    depth = x.shape[0]
    acc = p[0:1]
    rows = [acc]
    for l in range(1, depth):
        acc = acc + p[l:l + 1]
        rows.append(acc)
    cum = jnp.concatenate(rows, axis=0)
    o_ref[...] = cum - cum[0:1]


def _lower_bounds(lb_logits):
    return pl.pallas_call(
        _lower_bound_kernel,
        out_shape=jax.ShapeDtypeStruct(lb_logits.shape, F32),
        name="hgrn_lower_bounds",
    )(lb_logits)


def _rms(x):
    return x * lax.rsqrt(jnp.mean(x * x, axis=-1, keepdims=True) + EPS)


def _in_proj_kernel(x_ref, shift_ref, scale_ref, gain_ref, w_ref, z_ref):
    y = _rms(x_ref[...]) * gain_ref[...]
    y = y * (1.0 + scale_ref[...]) + shift_ref[...]
    z_ref[...] = jnp.dot(y.astype(BF16), w_ref[...], preferred_element_type=F32)


def _in_proj(x, shift, scale, gain, w, *, tm=256):
    bsz, s, d = x.shape
    nz = w.shape[-1]
    return pl.pallas_call(
        _in_proj_kernel,
        grid=(bsz, s // tm),
        in_specs=[pl.BlockSpec((None, tm, d), lambda b, i: (b, i, 0)),
                  pl.BlockSpec((None, 1, d), lambda b, i: (b, 0, 0)),
                  pl.BlockSpec((None, 1, d), lambda b, i: (b, 0, 0)),
                  pl.BlockSpec((1, d), lambda b, i: (0, 0)),
                  pl.BlockSpec((d, nz), lambda b, i: (0, 0))],
        out_specs=pl.BlockSpec((None, tm, nz), lambda b, i: (b, i, 0)),
        out_shape=jax.ShapeDtypeStruct((bsz, s, nz), F32),
        compiler_params=pltpu.CompilerParams(dimension_semantics=("arbitrary", "arbitrary")),
        name="norm_in_proj",
    )(x, shift, scale, gain, w)


def _stack_masks(bsz):
    r = np.arange(bsz * CHUNK)
    batch = (r // PIECE) % bsz
    pos = (r // (PIECE * bsz)) * PIECE + r % PIECE
    return ((batch[:, None] == batch[None, :]) & (pos[None, :] <= pos[:, None])).astype(np.float32)


def _pieces(x, bsz):
    n_groups = x.shape[0] // (PIECE * bsz)
    return [[x[(i * bsz + b) * PIECE:(i * bsz + b + 1) * PIECE] for b in range(bsz)] for i in range(n_groups)]


def _rows_of_batch(x, b, bsz):
    return jnp.concatenate([p[b] for p in _pieces(x, bsz)], axis=0)


def _per_batch_rows(rows, bsz, reps):
    slab = jnp.concatenate([jnp.broadcast_to(rows[b], (PIECE, rows[b].shape[-1])) for b in range(bsz)], axis=0)
    return jnp.concatenate([slab] * reps, axis=0) if reps > 1 else slab


def _halves(lane, even, odd):
    return jnp.where(lane < HEAD_DIM, even, odd)


def _head_sums(x, lane):
    lo = jnp.sum(jnp.where(lane < HEAD_DIM, x, 0.0), axis=-1, keepdims=True)
    hi = jnp.sum(jnp.where(lane < HEAD_DIM, 0.0, x), axis=-1, keepdims=True)
    return _halves(lane, lo, hi)


def _decayed_attention(q, k, g, v_tile, state_ref, mask, tri, bsz, *, heads, dk, n_groups):
    rows, wk = q.shape
    glen = CHUNK // n_groups
    gp = glen // PIECE
    grows = rows // n_groups
    b = _cumsum_rows(g, tri)
    bp = _pieces(b, bsz)

    def ref_rows(gi):
        mid = gi * glen + glen // 2
        return _per_batch_rows([bp[mid // PIECE][bb][mid % PIECE:mid % PIECE + 1] for bb in range(bsz)], bsz, gp)

    refs = [ref_rows(gi) for gi in range(n_groups)]
    q_in = q * jnp.exp(jnp.minimum(b - jnp.concatenate(refs, axis=0), MAX_EXP))
    q_state = q * jnp.exp(b)
    last = [bp[-1][bb][PIECE - 1:PIECE] for bb in range(bsz)]
    k_state = (k * jnp.exp(_per_batch_rows(last, bsz, rows // (bsz * PIECE)) - b)).astype(BF16)
    k_in = []
    for gi in range(n_groups):
        end = (gi + 1) * grows
        ref = jnp.concatenate([refs[gi]] * (gi + 1), axis=0) if gi else refs[gi]
        k_in.append((k[:end] * jnp.exp(jnp.minimum(ref - b[:end], MAX_EXP))).astype(BF16))

    lane = lax.broadcasted_iota(jnp.int32, (1, LANE), 1)
    heads_per_tile = LANE // dk
    n_pairs = heads // 2
    probs = [None] * heads
    for kt in range(-(-heads // heads_per_tile)):
        tile_heads = list(range(kt * heads_per_tile, min(heads, (kt + 1) * heads_per_tile)))
        tl = slice(kt * LANE, (kt + 1) * LANE)
        per_head = {h: [] for h in tile_heads}
        for gi in range(n_groups):
            end = (gi + 1) * grows
            qg = q_in[gi * grows:end, tl]
            lhs = jnp.concatenate(
                [jnp.where((lane >= (h % heads_per_tile) * dk) & (lane < (h % heads_per_tile + 1) * dk), qg, 0.0)
                 for h in tile_heads], axis=0).astype(BF16)
            sc = _dot_nt(lhs, k_in[gi][:, tl])
            for n, h in enumerate(tile_heads):
                blk = jnp.where(mask[gi * grows:end, :end], sc[n * grows:(n + 1) * grows], 0.0).astype(BF16)
                if end < rows:
                    blk = jnp.concatenate([blk, jnp.zeros((grows, rows - end), BF16)], axis=1)
                per_head[h].append(blk)
        for h in tile_heads:
            probs[h] = jnp.concatenate(per_head[h], axis=0)

    lane_batch = (lax.broadcasted_iota(jnp.int32, (1, rows), 1) // PIECE) % bsz
    row_i = lax.broadcasted_iota(jnp.int32, (LANE, LANE), 0)
    col_i = lax.broadcasted_iota(jnp.int32, (LANE, LANE), 1)
    outs = []
    for p in range(n_pairs):
        kt = (2 * p) // heads_per_tile
        tl = slice(kt * LANE, (kt + 1) * LANE)
        v = v_tile(p)
        vb = v.astype(BF16)
        intra = _halves(lane, _dot(probs[2 * p], vb), _dot(probs[2 * p + 1], vb))
        qs = _pieces(q_state[:, tl], bsz)
        inter_b = []
        for bb in range(bsz):
            lhs = jnp.concatenate([qs[i][bb] for i in range(len(qs))], axis=0)
            inter_b.append(_dot_nt(lhs, state_ref[p, bb]))
        inter = jnp.concatenate([inter_b[bb][i * PIECE:(i + 1) * PIECE]
                                 for i in range(len(qs)) for bb in range(bsz)], axis=0)
        outs.append(intra + inter)
        vt = v.T
        lhs = jnp.concatenate([jnp.where(lane_batch == bb, vt, 0.0) for bb in range(bsz)], axis=0)
        upd = _dot(lhs, k_state[:, tl])
        same_head = (2 * p + row_i // HEAD_DIM) == (kt * heads_per_tile + col_i // dk)
        for bb in range(bsz):
            decay = jnp.exp(last[bb][:, tl])
            state_ref[p, bb] = state_ref[p, bb] * decay + jnp.where(same_head, upd[bb * LANE:(bb + 1) * LANE], 0.0)
    return outs


def _mixer_kernel(z_ref, mask_ref, tri_ref, wgate_ref, bgate_ref, gnorm_ref, lb_ref, hnorm_ref, conv_ref,
                  gateb_ref, mnorm_ref, o_ref, gla_state, hgrn_state, mlstm_state, mlstm_m, conv_carry, *, bsz):
    @pl.when(pl.program_id(0) == 0)
    def _():
        gla_state[...] = jnp.zeros_like(gla_state)
        hgrn_state[...] = jnp.zeros_like(hgrn_state)
        mlstm_state[...] = jnp.zeros_like(mlstm_state)
        mlstm_m[...] = jnp.zeros_like(mlstm_m)
        conv_carry[...] = jnp.zeros_like(conv_carry)

    n_pieces = CHUNK // PIECE
    rows = bsz * CHUNK
    mask = mask_ref[...] > 0.5
    tri = tri_ref[...]
    lane = lax.broadcasted_iota(jnp.int32, (1, LANE), 1)

    def load(off, width):
        return jnp.concatenate([z_ref[b, i, :, off:off + width] for i in range(n_pieces) for b in range(bsz)], axis=0)

    def store(tile_idx, val):
        val = val.astype(o_ref.dtype)
        for i in range(n_pieces):
            for b in range(bsz):
                r0 = (i * bsz + b) * PIECE
                o_ref[b, i, :, tile_idx * LANE:(tile_idx + 1) * LANE] = val[r0:r0 + PIECE]

    small = load(SMALL, LANE)

    logits = _dot(small, wgate_ref[...]) + bgate_ref[...]
    g_gla = _log_sigmoid(logits) * (1.0 / GLA_TAU)
    q_gla = load(OFF["gq"], GLA_QK_PAD) * (GLA_DK ** -0.5)
    k_gla = load(OFF["gk"], GLA_QK_PAD)
    o_gla = _decayed_attention(q_gla, k_gla, g_gla, lambda p: load(OFF["gv"] + p * LANE, LANE), gla_state,
                               mask, tri, bsz, heads=GLA_HEADS, dk=GLA_DK, n_groups=4)
    for p in range(GLA_HEADS // 2):
        o = o_gla[p]
        o = o * lax.rsqrt(_head_sums(o * o, lane) * (1.0 / HEAD_DIM) + EPS) * gnorm_ref[...]
        o = o * _silu(load(OFF["gout"] + p * LANE, LANE))
        store(p, o)

    lb = lb_ref[...]
    f = load(OFF["hf"], HGRN_W)
    log_lb = jnp.log(lb)
    log_rest = jnp.log1p(-lb) + _log_sigmoid(f)
    g_hgrn = jnp.maximum(log_lb, log_rest) + jnp.log1p(jnp.exp(-jnp.abs(log_lb - log_rest)))
    k_hgrn = (1.0 - lb) * jax.nn.sigmoid(-f)
    q_hgrn = _silu(load(OFF["hq"], HGRN_W))
    o_hgrn = _decayed_attention(q_hgrn, k_hgrn, g_hgrn, lambda p: load(OFF["hi"] + p * LANE, LANE), hgrn_state,
                                mask, tri, bsz, heads=HGRN_HEADS, dk=HGRN_DK, n_groups=4)
    for p in range(HGRN_HEADS // 2):
        o = o_hgrn[p]
        o = o * lax.rsqrt(_head_sums(o * o, lane) * (1.0 / HEAD_DIM) + EPS) * hnorm_ref[...]
        o = o * jax.nn.sigmoid(load(OFF["hout"] + p * LANE, LANE))
        store(GLA_W // LANE + p, o)

    conv_b = []
    for b in range(bsz):
        x_qk = jnp.concatenate([z_ref[b, i, :, OFF["mqk"]:OFF["mqk"] + 2 * MLSTM_W] for i in range(n_pieces)], axis=0)
        hist = jnp.concatenate([conv_carry[b], x_qk], axis=0)
        conv_carry[b] = x_qk[CHUNK - 8:CHUNK, :]
        lo = 8 - (CONV_WIDTH - 1)
        acc = conv_ref[0:1, :] * hist[lo:lo + CHUNK, :]
        for t in range(1, CONV_WIDTH):
            acc = acc + conv_ref[t:t + 1, :] * hist[lo + t:lo + t + CHUNK, :]
        conv_b.append(acc)
    qk = _silu(jnp.concatenate([conv_b[b][i * PIECE:(i + 1) * PIECE] for i in range(n_pieces) for b in range(bsz)],
                               axis=0))

    pre = small + gateb_ref[...]
    gates = jnp.where(lane < MLSTM_HEADS, _log_sigmoid(pre), jnp.where(lane < 2 * MLSTM_HEADS, pre, 0.0))
    cum = _cumsum_rows(gates, tri)
    row_form = jnp.where(lane < MLSTM_HEADS, cum, gates).T
    lane_r = lax.broadcasted_iota(jnp.int32, (1, rows), 1)
    lane_batch = (lane_r // PIECE) % bsz
    row_batch = (lax.broadcasted_iota(jnp.int32, (rows, 1), 0) // PIECE) % bsz
    ones_tile = jnp.ones((rows, LANE), BF16)
    for p in range(MLSTM_HEADS // 2):
        q_pair = qk[:, p * LANE:(p + 1) * LANE]
        k_pair = qk[:, MLSTM_W + p * LANE:MLSTM_W + (p + 1) * LANE] * (HEAD_DIM ** -0.5)
        kt_pair = k_pair.T
        v2 = jnp.concatenate([load(OFF["mv"] + p * LANE, LANE).astype(BF16), ones_tile], axis=1)
        y_heads = []
        for hh in range(2):
            h = 2 * p + hh
            own = (lane >= hh * HEAD_DIM) & (lane < (hh + 1) * HEAD_DIM)
            b_row = row_form[h:h + 1, :]
            c_row = row_form[MLSTM_HEADS + h:MLSTM_HEADS + h + 1, :] - b_row
            b_col = jnp.sum(jnp.where(lane == h, cum, 0.0), axis=-1, keepdims=True)
            m_prev = [mlstm_m[bb:bb + 1, h:h + 1] for bb in range(bsz)]
            m_prev_col = m_prev[0]
            for bb in range(1, bsz):
                m_prev_col = jnp.where(row_batch == bb, m_prev[bb], m_prev_col)
            c_max = jnp.max(jnp.where(mask, c_row, NEG), axis=-1, keepdims=True)
            mu = jnp.maximum(m_prev_col, c_max)
            q_own = jnp.where(own, q_pair, 0.0)
            prob = jnp.exp(jnp.where(mask, c_row - mu, NEG)) * _dot_nt(q_own, k_pair)
            res = _dot(prob, v2)
            qo = _pieces(q_own, bsz)
            inter_b = []
            for bb in range(bsz):
                lhs = jnp.concatenate([qo[i][bb] for i in range(n_pieces)], axis=0)
                inter_b.append(_dot(lhs, mlstm_state[h, bb]))
            inter = jnp.concatenate([inter_b[bb][i * PIECE:(i + 1) * PIECE]
                                     for i in range(n_pieces) for bb in range(bsz)], axis=0)
            res = res + jnp.exp(m_prev_col - mu) * inter
            den = jnp.maximum(jnp.abs(res[:, LANE:]), jnp.exp(-(b_col + mu)))
            y_heads.append(res[:, :LANE] / den)
            b_end = [b_col[rows - (bsz - bb) * PIECE + PIECE - 1:rows - (bsz - bb) * PIECE + PIECE, :]
                     for bb in range(bsz)]
            b_end_row = b_end[0]
            for bb in range(1, bsz):
                b_end_row = jnp.where(lane_batch == bb, b_end[bb], b_end_row)
            w_row = b_end_row + c_row
            m_loc = [jnp.max(jnp.where(lane_batch == bb, w_row, NEG), axis=-1, keepdims=True) for bb in range(bsz)]
            m_loc_row = m_loc[0]
            for bb in range(1, bsz):
                m_loc_row = jnp.where(lane_batch == bb, m_loc[bb], m_loc_row)
            ks_t = kt_pair * jnp.exp(w_row - m_loc_row)
            lhs = jnp.concatenate([jnp.where(lane_batch == bb, ks_t, 0.0) for bb in range(bsz)], axis=0)
            upd = _dot(lhs, v2)
            for bb in range(bsz):
                m_new = jnp.maximum(b_end[bb] + m_prev[bb], m_loc[bb])
                s_old = jnp.exp(b_end[bb] + m_prev[bb] - m_new)
                s_loc = jnp.exp(m_loc[bb] - m_new)
                mlstm_state[h, bb] = s_old * mlstm_state[h, bb] + s_loc * upd[bb * LANE:(bb + 1) * LANE]
                mlstm_m[bb:bb + 1, h:h + 1] = m_new
        y = _halves(lane, y_heads[0], y_heads[1])
        mu_h = _head_sums(y, lane) * (1.0 / HEAD_DIM)
        yc = y - mu_h
        var = _head_sums(yc * yc, lane) * (1.0 / HEAD_DIM)
        y = yc * lax.rsqrt(var + EPS) * mnorm_ref[:, p * LANE:(p + 1) * LANE]
        y = y * jax.nn.sigmoid(load(OFF["mout"] + p * LANE, LANE))
        store((GLA_W + HGRN_W) // LANE + p, y)


def _mixers(z, wgate, bgate, gnorm, lb, hnorm, conv_w, gate_b, mnorm):
    bsz, s, nz = z.shape
    d_out = GLA_W + HGRN_W + MLSTM_W
    n_chunks = s // CHUNK
    n_pieces = CHUNK // PIECE
    rows = bsz * CHUNK
    mask = jnp.asarray(_stack_masks(bsz))
    const = lambda c: (0, 0)
    out = pl.pallas_call(
        functools.partial(_mixer_kernel, bsz=bsz),
        grid=(n_chunks,),
        in_specs=[pl.BlockSpec((bsz, None, n_pieces, PIECE, nz), lambda c: (0, c, 0, 0, 0)),
                  pl.BlockSpec((rows, rows), const), pl.BlockSpec((rows, rows), const),
                  pl.BlockSpec(wgate.shape, const), pl.BlockSpec(bgate.shape, const),
                  pl.BlockSpec(gnorm.shape, const), pl.BlockSpec(lb.shape, const),
                  pl.BlockSpec(hnorm.shape, const), pl.BlockSpec(conv_w.shape, const),
                  pl.BlockSpec(gate_b.shape, const), pl.BlockSpec(mnorm.shape, const)],
        out_specs=pl.BlockSpec((bsz, None, n_pieces, PIECE, d_out), lambda c: (0, c, 0, 0, 0)),
        out_shape=jax.ShapeDtypeStruct((bsz, n_chunks, n_pieces, PIECE, d_out), BF16),
        scratch_shapes=[pltpu.VMEM((GLA_HEADS // 2, bsz, LANE, LANE), F32),
                        pltpu.VMEM((HGRN_HEADS // 2, bsz, LANE, LANE), F32),
                        pltpu.VMEM((MLSTM_HEADS, bsz, LANE, 2 * LANE), F32),
                        pltpu.VMEM((8, LANE), F32),
                        pltpu.VMEM((bsz, 8, 2 * MLSTM_W), F32)],
        compiler_params=pltpu.CompilerParams(dimension_semantics=("arbitrary",),
                                             vmem_limit_bytes=48 * 1024 * 1024),
        name="token_mixers",
    )(z.reshape(bsz, n_chunks, n_pieces, PIECE, nz), mask, mask.astype(BF16),
      wgate, bgate, gnorm, lb, hnorm, conv_w, gate_b, mnorm)
    return out.reshape(bsz, s, d_out)


def _out_mlp_kernel(mixed_ref, x_ref, gate1_ref, shift_ref, scale_ref, gate2_ref, gain_ref,
                    wout_ref, w1_ref, w2_ref, fnorm_ref, o_ref, *, final):
    x = x_ref[...] + gate1_ref[...] * jnp.dot(mixed_ref[...], wout_ref[...], preferred_element_type=F32)
    y = _rms(x) * gain_ref[...]
    y = y * (1.0 + scale_ref[...]) + shift_ref[...]
    u = jnp.dot(y.astype(BF16), w1_ref[...], preferred_element_type=F32)
    u = jnp.square(jnp.maximum(u, 0.0))
    x = x + gate2_ref[...] * jnp.dot(u.astype(BF16), w2_ref[...], preferred_element_type=F32)
    if final:
        x = _rms(x) * fnorm_ref[...]
    o_ref[...] = x


def _out_mlp(mixed, x, gate1, shift, scale, gate2, gain, w_out, w1, w2, fnorm, *, final, tm=256):
    bsz, s, d = x.shape
    dff = w1.shape[-1]
    tok = pl.BlockSpec((None, tm, d), lambda b, i: (b, i, 0))
    mod = pl.BlockSpec((None, 1, d), lambda b, i: (b, 0, 0))
    const = lambda b, i: (0, 0)
    return pl.pallas_call(
        functools.partial(_out_mlp_kernel, final=final),
        grid=(bsz, s // tm),
        in_specs=[tok, tok, mod, mod, mod, mod, pl.BlockSpec((1, d), const),
                  pl.BlockSpec((d, d), const), pl.BlockSpec((d, dff), const), pl.BlockSpec((dff, d), const),
                  pl.BlockSpec((1, d), const)],
        out_specs=tok,
        out_shape=jax.ShapeDtypeStruct((bsz, s, d), F32),
        compiler_params=pltpu.CompilerParams(dimension_semantics=("arbitrary", "arbitrary"),
                                             vmem_limit_bytes=56 * 1024 * 1024),
        name="out_proj_mlp",
    )(mixed, x, gate1, shift, scale, gate2, gain, w_out, w1, w2, fnorm)


def kernel(x, c, w_ada, b_ada, norm_mix, norm_mlp, w_in, gla_w_gate, gla_b_gate, gla_norm, hgrn_lb_logits,
           hgrn_norm, mlstm_conv, mlstm_gate_b, mlstm_norm, w_out, w_ff1, w_ff2, final_norm):
    depth, d, _ = w_ada.shape
    bsz = x.shape[0]
    mod = _adaln(c.astype(F32), w_ada.astype(F32), b_ada.astype(F32))
    mod = mod.reshape(depth, bsz, 6, 1, d)
    lower = _lower_bounds(hgrn_lb_logits.astype(F32))
    w_in_p = _permute_in_proj(w_in)
    gate_b = jnp.concatenate([mlstm_gate_b[:, MLSTM_HEADS:], mlstm_gate_b[:, :MLSTM_HEADS],
                              jnp.zeros((depth, LANE - 2 * MLSTM_HEADS), F32)], axis=-1)
    wgate = jnp.zeros((depth, LANE, GLA_QK_PAD), F32)
    wgate = wgate.at[:, GLOW_LANE:GLOW_LANE + GLA_RANK, :GLA_QK].set(gla_w_gate).astype(BF16)
    bgate = jnp.pad(gla_b_gate, ((0, 0), (0, GLA_QK_PAD - GLA_QK)))
    gnorm2 = jnp.concatenate([gla_norm, gla_norm], axis=-1)
    hnorm2 = jnp.concatenate([hgrn_norm, hgrn_norm], axis=-1)
    w_out_b, w1_b, w2_b = w_out.astype(BF16), w_ff1.astype(BF16), w_ff2.astype(BF16)
    fnorm = final_norm.reshape(1, d).astype(F32)
    for l in range(depth):
        shift1, scale1, gate1, shift2, scale2, gate2 = (mod[l, :, j] for j in range(6))
        z = _in_proj(x, shift1, scale1, norm_mix[l].reshape(1, d), w_in_p[l])
        mixed = _mixers(z, wgate[l], bgate[l].reshape(1, -1), gnorm2[l].reshape(1, -1),
                        lower[l].reshape(1, -1), hnorm2[l].reshape(1, -1), mlstm_conv[l],
                        gate_b[l].reshape(1, -1), mlstm_norm[l].reshape(1, -1))
        x = _out_mlp(mixed, x, gate1, shift2, scale2, gate2, norm_mlp[l].reshape(1, d),
                     w_out_b[l], w1_b[l], w2_b[l], fnorm, final=(l == depth - 1))
    return x
```

```python
import functools

import numpy as np
import jax
import jax.numpy as jnp
from jax import lax
from jax.experimental import pallas as pl
from jax.experimental.pallas import tpu as pltpu

F32 = jnp.float32
BF16 = jnp.bfloat16

HEAD_DIM = 64
GLA_HEADS = 6
GLA_DK = 32
GLA_RANK = 16
GLA_TAU = 16.0
HGRN_HEADS = 6
HGRN_DK = 64
MLSTM_HEADS = 4
CONV_WIDTH = 4
EPS = 1e-6
CHUNK = 64
PIECE = 16
LANE = 128
NEG = -1e30
MAX_EXP2 = 115.0
LOG2E = 1.4426950408889634

GLA_QK = GLA_HEADS * GLA_DK
GLA_W = GLA_HEADS * HEAD_DIM
HGRN_W = HGRN_HEADS * HEAD_DIM
MLSTM_W = MLSTM_HEADS * HEAD_DIM
GLA_QK_PAD = 2 * LANE

_REF_SEGMENTS = (("gq", GLA_QK), ("gk", GLA_QK), ("gv", GLA_W), ("glow", GLA_RANK), ("gout", GLA_W),
                 ("hq", HGRN_W), ("hf", HGRN_W), ("hi", HGRN_W), ("hout", HGRN_W),
                 ("mqk", 2 * MLSTM_W), ("mv", MLSTM_W), ("mi", MLSTM_HEADS), ("mf", MLSTM_HEADS),
                 ("mout", MLSTM_W))
_MY_ORDER = (("gq", 0), ("pad", GLA_QK_PAD - GLA_QK), ("gk", 0), ("pad", GLA_QK_PAD - GLA_QK),
             ("gv", 0), ("gout", 0), ("hq", 0), ("hf", 0), ("hi", 0), ("hout", 0),
             ("mqk", 0), ("mv", 0), ("mout", 0), ("mf", 0), ("mi", 0), ("glow", 0))


def _layout():
    ref_off, off = {}, 0
    for name, width in _REF_SEGMENTS:
        ref_off[name] = (off, width)
        off += width
    my_off, off = {}, 0
    for name, pad in _MY_ORDER:
        if name == "pad":
            off += pad
        else:
            my_off[name] = off
            off += ref_off[name][1]
    padded = -(-off // LANE) * LANE
    return ref_off, my_off, off, padded


_REF_OFF, OFF, _NZ_USED, NZ = _layout()
SMALL = OFF["mf"]
GLOW_LANE = OFF["glow"] - SMALL
assert SMALL % LANE == 0 and OFF["mi"] == SMALL + MLSTM_HEADS and GLOW_LANE == 2 * MLSTM_HEADS
assert all(OFF[n] % LANE == 0 for n in ("gq", "gk", "gv", "gout", "hq", "hf", "hi", "hout", "mqk", "mv", "mout"))


def _permute_kernel(w_ref, o_ref):
    off = 0
    for name, pad in _MY_ORDER:
        if name == "pad":
            o_ref[:, off:off + pad] = jnp.zeros((o_ref.shape[0], pad), o_ref.dtype)
            off += pad
        else:
            lo, width = _REF_OFF[name]
            o_ref[:, off:off + width] = w_ref[:, lo:lo + width].astype(o_ref.dtype)
            off += width
    o_ref[:, off:] = jnp.zeros((o_ref.shape[0], o_ref.shape[1] - off), o_ref.dtype)


def _permute_in_proj(w_in, *, tr=256):
    depth, d, d_in = w_in.shape
    return pl.pallas_call(
        _permute_kernel,
        grid=(depth, d // tr),
        in_specs=[pl.BlockSpec((None, tr, d_in), lambda l, i: (l, i, 0))],
        out_specs=pl.BlockSpec((None, tr, NZ), lambda l, i: (l, i, 0)),
        out_shape=jax.ShapeDtypeStruct((depth, d, NZ), BF16),
        name="permute_in_proj_weight",
    )(w_in)


def _silu(x):
    return x * jax.nn.sigmoid(x)


def _log2_sigmoid(x):
    return jnp.minimum(x, 0.0) * LOG2E - jnp.log2(1.0 + jnp.exp2(jnp.abs(x) * (-LOG2E)))


def _dot(a, b):
    return jnp.dot(a.astype(BF16), b.astype(BF16), preferred_element_type=F32)


def _dot_nt(a, b):
    return lax.dot_general(a.astype(BF16), b.astype(BF16), (((1,), (1,)), ((), ())),
                           preferred_element_type=F32)


def _cumsum_rows(g, tri):
    hi = g.astype(BF16)
    lo = (g - hi.astype(F32)).astype(BF16)
    return (jnp.dot(tri, hi, preferred_element_type=F32) + jnp.dot(tri, lo, preferred_element_type=F32))


def _adaln_kernel(c_ref, w_ref, b_ref, o_ref):
    cond = _silu(c_ref[...])
    o_ref[...] = jnp.dot(cond, w_ref[...], preferred_element_type=F32,
                         precision=lax.Precision.HIGHEST) + b_ref[...]


def _adaln(c, w_ada, b_ada, *, tn=2048):
    depth, d, n = w_ada.shape
    bsz = c.shape[0]
    return pl.pallas_call(
        _adaln_kernel,
        grid=(depth, n // tn),
        in_specs=[pl.BlockSpec((bsz, d), lambda l, j: (0, 0)),
                  pl.BlockSpec((None, d, tn), lambda l, j: (l, 0, j)),
                  pl.BlockSpec((None, 1, tn), lambda l, j: (l, 0, j))],
        out_specs=pl.BlockSpec((None, bsz, tn), lambda l, j: (l, 0, j)),
        out_shape=jax.ShapeDtypeStruct((depth, bsz, n), F32),
        name="adaln_modulation",
    )(c, w_ada, b_ada.reshape(depth, 1, n))


def _lower_bound_kernel(logit_ref, o_ref):
    x = logit_ref[...]
    e = jnp.exp(x - jnp.max(x, axis=0, keepdims=True))
    p = e / jnp.sum(e, axis=0, keepdims=True)
    depth = x.shape[0]
    acc = p[0:1]
    rows = [acc]
    for l in range(1, depth):
        acc = acc + p[l:l + 1]
        rows.append(acc)
    cum = jnp.concatenate(rows, axis=0)
    o_ref[...] = cum - cum[0:1]


def _lower_bounds(lb_logits):
    return pl.pallas_call(
        _lower_bound_kernel,
        out_shape=jax.ShapeDtypeStruct(lb_logits.shape, F32),
        name="hgrn_lower_bounds",
    )(lb_logits)


def _rms(x):
    return x * lax.rsqrt(jnp.mean(x * x, axis=-1, keepdims=True) + EPS)


def _in_proj_kernel(x_ref, shift_ref, scale_ref, gain_ref, w_ref, z_ref):
    y = _rms(x_ref[...]) * gain_ref[...]
    y = y * (1.0 + scale_ref[...]) + shift_ref[...]
    z_ref[...] = jnp.dot(y.astype(BF16), w_ref[...], preferred_element_type=F32)


def _resident(shape):
    return pl.BlockSpec(shape, lambda b, i: (0,) * len(shape), pipeline_mode=pl.Buffered(1))


def _in_proj(x, shift, scale, gain, w, *, tm=512):
    bsz, s, d = x.shape
    nz = w.shape[-1]
    return pl.pallas_call(
        _in_proj_kernel,
        grid=(bsz, s // tm),
        in_specs=[pl.BlockSpec((None, tm, d), lambda b, i: (b, i, 0)),
                  pl.BlockSpec((None, 1, d), lambda b, i: (b, 0, 0)),
                  pl.BlockSpec((None, 1, d), lambda b, i: (b, 0, 0)),
                  _resident((1, d)),
                  _resident((d, nz))],
        out_specs=pl.BlockSpec((None, tm, nz), lambda b, i: (b, i, 0)),
        out_shape=jax.ShapeDtypeStruct((bsz, s, nz), F32),
        compiler_params=pltpu.CompilerParams(dimension_semantics=("arbitrary", "arbitrary"),
                                             vmem_limit_bytes=48 * 1024 * 1024),
        name="norm_in_proj",
    )(x, shift, scale, gain, w)


def _stack_masks(bsz):
    r = np.arange(bsz * CHUNK)
    batch = (r // PIECE) % bsz
    pos = (r // (PIECE * bsz)) * PIECE + r % PIECE
    return ((batch[:, None] == batch[None, :]) & (pos[None, :] <= pos[:, None])).astype(np.float32)


def _pieces(x, bsz):
    n_groups = x.shape[0] // (PIECE * bsz)
    return [[x[(i * bsz + b) * PIECE:(i * bsz + b + 1) * PIECE] for b in range(bsz)] for i in range(n_groups)]


def _rows_of_batch(x, b, bsz):
    return jnp.concatenate([p[b] for p in _pieces(x, bsz)], axis=0)


def _per_batch_rows(rows, bsz, reps):
    slab = jnp.concatenate([jnp.broadcast_to(rows[b], (PIECE, rows[b].shape[-1])) for b in range(bsz)], axis=0)
    return jnp.concatenate([slab] * reps, axis=0) if reps > 1 else slab


def _halves(lane, even, odd):
    return jnp.where(lane < HEAD_DIM, even, odd)


def _head_sums(x, lane):
    lo = jnp.sum(jnp.where(lane < HEAD_DIM, x, 0.0), axis=-1, keepdims=True)
    hi = jnp.sum(jnp.where(lane < HEAD_DIM, 0.0, x), axis=-1, keepdims=True)
    return _halves(lane, lo, hi)


def _decayed_attention(q, k, g, v_tile, state_ref, mask, tri, bsz, *, heads, dk, n_groups):
    rows, wk = q.shape
    glen = CHUNK // n_groups
    gp = glen // PIECE
    grows = rows // n_groups
    b = _cumsum_rows(g, tri)
    bp = _pieces(b, bsz)

    def ref_rows(gi):
        mid = gi * glen + glen // 2
        return _per_batch_rows([bp[mid // PIECE][bb][mid % PIECE:mid % PIECE + 1] for bb in range(bsz)], bsz, gp)

    refs = [ref_rows(gi) for gi in range(n_groups)]
    q_in = q * jnp.exp2(jnp.minimum(b - jnp.concatenate(refs, axis=0), MAX_EXP2))
    q_state = q * jnp.exp2(b)
    last = [bp[-1][bb][PIECE - 1:PIECE] for bb in range(bsz)]
    k_state = (k * jnp.exp2(_per_batch_rows(last, bsz, rows // (bsz * PIECE)) - b)).astype(BF16)
    k_in = []
    for gi in range(n_groups):
        end = (gi + 1) * grows
        ref = jnp.concatenate([refs[gi]] * (gi + 1), axis=0) if gi else refs[gi]
        k_in.append((k[:end] * jnp.exp2(jnp.minimum(ref - b[:end], MAX_EXP2))).astype(BF16))

    lane = lax.broadcasted_iota(jnp.int32, (1, LANE), 1)
    heads_per_tile = LANE // dk
    n_pairs = heads // 2
    probs = [None] * heads
    for kt in range(-(-heads // heads_per_tile)):
        tile_heads = list(range(kt * heads_per_tile, min(heads, (kt + 1) * heads_per_tile)))
        tl = slice(kt * LANE, (kt + 1) * LANE)
        per_head = {h: [] for h in tile_heads}
        for gi in range(n_groups):
            end = (gi + 1) * grows
            qg = q_in[gi * grows:end, tl]
            lhs = jnp.concatenate(
                [jnp.where((lane >= (h % heads_per_tile) * dk) & (lane < (h % heads_per_tile + 1) * dk), qg, 0.0)
                 for h in tile_heads], axis=0).astype(BF16)
            sc = _dot_nt(lhs, k_in[gi][:, tl])
            for n, h in enumerate(tile_heads):
                blk = jnp.where(mask[gi * grows:end, :end], sc[n * grows:(n + 1) * grows], 0.0).astype(BF16)
                if end < rows:
                    blk = jnp.concatenate([blk, jnp.zeros((grows, rows - end), BF16)], axis=1)
                per_head[h].append(blk)
        for h in tile_heads:
            probs[h] = jnp.concatenate(per_head[h], axis=0)

    lane_batch = (lax.broadcasted_iota(jnp.int32, (1, rows), 1) // PIECE) % bsz
    row_i = lax.broadcasted_iota(jnp.int32, (LANE, LANE), 0)
    col_i = lax.broadcasted_iota(jnp.int32, (LANE, LANE), 1)
    outs = []
    for p in range(n_pairs):
        kt = (2 * p) // heads_per_tile
        tl = slice(kt * LANE, (kt + 1) * LANE)
        v = v_tile(p)
        vb = v.astype(BF16)
        intra = _halves(lane, _dot(probs[2 * p], vb), _dot(probs[2 * p + 1], vb))
        qs = _pieces(q_state[:, tl], bsz)
        inter_b = []
        for bb in range(bsz):
            lhs = jnp.concatenate([qs[i][bb] for i in range(len(qs))], axis=0)
            inter_b.append(_dot_nt(lhs, state_ref[p, bb]))
        inter = jnp.concatenate([inter_b[bb][i * PIECE:(i + 1) * PIECE]
                                 for i in range(len(qs)) for bb in range(bsz)], axis=0)
        outs.append(intra + inter)
        vt = v.T
        lhs = jnp.concatenate([jnp.where(lane_batch == bb, vt, 0.0) for bb in range(bsz)], axis=0)
        upd = _dot(lhs, k_state[:, tl])
        same_head = (2 * p + row_i // HEAD_DIM) == (kt * heads_per_tile + col_i // dk)
        for bb in range(bsz):
            decay = jnp.exp2(last[bb][:, tl])
            state_ref[p, bb] = state_ref[p, bb] * decay + jnp.where(same_head, upd[bb * LANE:(bb + 1) * LANE], 0.0)
    return outs


def _mixer_kernel(z_ref, mask_ref, tri_ref, wgate_ref, bgate_ref, gnorm_ref, lb_ref, hnorm_ref, conv_ref,
                  gateb_ref, mnorm_ref, o_ref, gla_state, hgrn_state, mlstm_state, mlstm_m, conv_carry, *, bsz):
    @pl.when(pl.program_id(0) == 0)
    def _():
        gla_state[...] = jnp.zeros_like(gla_state)
        hgrn_state[...] = jnp.zeros_like(hgrn_state)
        mlstm_state[...] = jnp.zeros_like(mlstm_state)
        mlstm_m[...] = jnp.zeros_like(mlstm_m)
        conv_carry[...] = jnp.zeros_like(conv_carry)

    n_pieces = CHUNK // PIECE
    rows = bsz * CHUNK
    mask = mask_ref[...] > 0.5
    tri = tri_ref[...]
    lane = lax.broadcasted_iota(jnp.int32, (1, LANE), 1)

    def load(off, width):
        return jnp.concatenate([z_ref[b, i, :, off:off + width] for i in range(n_pieces) for b in range(bsz)], axis=0)

    def store(tile_idx, val):
        val = val.astype(o_ref.dtype)
        for i in range(n_pieces):
            for b in range(bsz):
                r0 = (i * bsz + b) * PIECE
                o_ref[b, i, :, tile_idx * LANE:(tile_idx + 1) * LANE] = val[r0:r0 + PIECE]

    small = load(SMALL, LANE)

    logits = _dot(small, wgate_ref[...]) + bgate_ref[...]
    g_gla = _log2_sigmoid(logits) * (1.0 / GLA_TAU)
    q_gla = load(OFF["gq"], GLA_QK_PAD) * (GLA_DK ** -0.5)
    k_gla = load(OFF["gk"], GLA_QK_PAD)
    o_gla = _decayed_attention(q_gla, k_gla, g_gla, lambda p: load(OFF["gv"] + p * LANE, LANE), gla_state,
                               mask, tri, bsz, heads=GLA_HEADS, dk=GLA_DK, n_groups=4)
    for p in range(GLA_HEADS // 2):
        o = o_gla[p]
        o = o * lax.rsqrt(_head_sums(o * o, lane) * (1.0 / HEAD_DIM) + EPS) * gnorm_ref[...]
        o = o * _silu(load(OFF["gout"] + p * LANE, LANE))
        store(p, o)

    lb = lb_ref[...]
    f = load(OFF["hf"], HGRN_W)
    t = jnp.exp2(jnp.abs(f) * (-LOG2E))
    u = 1.0 + t
    log_lb = jnp.log2(lb)
    log_rest = jnp.log2(1.0 - lb) + (jnp.minimum(f, 0.0) * LOG2E - jnp.log2(u))
    g_hgrn = jnp.maximum(log_lb, log_rest) + jnp.log2(1.0 + jnp.exp2(-jnp.abs(log_lb - log_rest)))
    k_hgrn = (1.0 - lb) * (jnp.where(f > 0.0, t, 1.0) / u)
    q_hgrn = _silu(load(OFF["hq"], HGRN_W))
    o_hgrn = _decayed_attention(q_hgrn, k_hgrn, g_hgrn, lambda p: load(OFF["hi"] + p * LANE, LANE), hgrn_state,
                                mask, tri, bsz, heads=HGRN_HEADS, dk=HGRN_DK, n_groups=4)
    for p in range(HGRN_HEADS // 2):
        o = o_hgrn[p]
        o = o * lax.rsqrt(_head_sums(o * o, lane) * (1.0 / HEAD_DIM) + EPS) * hnorm_ref[...]
        o = o * jax.nn.sigmoid(load(OFF["hout"] + p * LANE, LANE))
        store(GLA_W // LANE + p, o)

    conv_b = []
    for b in range(bsz):
        x_qk = jnp.concatenate([z_ref[b, i, :, OFF["mqk"]:OFF["mqk"] + 2 * MLSTM_W] for i in range(n_pieces)], axis=0)
        hist = jnp.concatenate([conv_carry[b], x_qk], axis=0)
        conv_carry[b] = x_qk[CHUNK - 8:CHUNK, :]
        lo = 8 - (CONV_WIDTH - 1)
        acc = conv_ref[0:1, :] * hist[lo:lo + CHUNK, :]
        for tap in range(1, CONV_WIDTH):
            acc = acc + conv_ref[tap:tap + 1, :] * hist[lo + tap:lo + tap + CHUNK, :]
        conv_b.append(acc)
    qk = _silu(jnp.concatenate([conv_b[b][i * PIECE:(i + 1) * PIECE] for i in range(n_pieces) for b in range(bsz)],
                               axis=0))

    pre = small + gateb_ref[...]
    gates = jnp.where(lane < MLSTM_HEADS, _log2_sigmoid(pre), jnp.where(lane < 2 * MLSTM_HEADS, pre * LOG2E, 0.0))
    cum = _cumsum_rows(gates, tri)
    row_form = jnp.where(lane < MLSTM_HEADS, cum, gates).T
    lane_r = lax.broadcasted_iota(jnp.int32, (1, rows), 1)
    lane_batch = (lane_r // PIECE) % bsz
    row_batch = (lax.broadcasted_iota(jnp.int32, (rows, 1), 0) // PIECE) % bsz
    ones_tile = jnp.ones((rows, LANE), BF16)
    for p in range(MLSTM_HEADS // 2):
        q_pair = qk[:, p * LANE:(p + 1) * LANE]
        k_pair = qk[:, MLSTM_W + p * LANE:MLSTM_W + (p + 1) * LANE] * (HEAD_DIM ** -0.5)
        kt_pair = k_pair.T
        v2 = jnp.concatenate([load(OFF["mv"] + p * LANE, LANE).astype(BF16), ones_tile], axis=1)
        y_heads = []
        for hh in range(2):
            h = 2 * p + hh
            own = (lane >= hh * HEAD_DIM) & (lane < (hh + 1) * HEAD_DIM)
            b_row = row_form[h:h + 1, :]
            c_row = row_form[MLSTM_HEADS + h:MLSTM_HEADS + h + 1, :] - b_row
            b_col = jnp.sum(jnp.where(lane == h, cum, 0.0), axis=-1, keepdims=True)
            m_prev = [mlstm_m[bb:bb + 1, h:h + 1] for bb in range(bsz)]
            m_prev_col = m_prev[0]
            for bb in range(1, bsz):
                m_prev_col = jnp.where(row_batch == bb, m_prev[bb], m_prev_col)
            c_max = jnp.max(jnp.where(mask, c_row, NEG), axis=-1, keepdims=True)
            mu = jnp.maximum(m_prev_col, c_max)
            q_own = jnp.where(own, q_pair, 0.0)
            prob = jnp.exp2(jnp.where(mask, c_row - mu, NEG)) * _dot_nt(q_own, k_pair)
            res = _dot(prob, v2)
            qo = _pieces(q_own, bsz)
            inter_b = []
            for bb in range(bsz):
                lhs = jnp.concatenate([qo[i][bb] for i in range(n_pieces)], axis=0)
                inter_b.append(_dot(lhs, mlstm_state[h, bb]))
            inter = jnp.concatenate([inter_b[bb][i * PIECE:(i + 1) * PIECE]
                                     for i in range(n_pieces) for bb in range(bsz)], axis=0)
            res = res + jnp.exp2(m_prev_col - mu) * inter
            den = jnp.maximum(jnp.abs(res[:, LANE:]), jnp.exp2(-(b_col + mu)))
            y_heads.append(res[:, :LANE] / den)
            b_end = [b_col[rows - (bsz - bb) * PIECE + PIECE - 1:rows - (bsz - bb) * PIECE + PIECE, :]
                     for bb in range(bsz)]
            b_end_row = b_end[0]
            for bb in range(1, bsz):
                b_end_row = jnp.where(lane_batch == bb, b_end[bb], b_end_row)
            w_row = b_end_row + c_row
            m_loc = [jnp.max(jnp.where(lane_batch == bb, w_row, NEG), axis=-1, keepdims=True) for bb in range(bsz)]
            m_loc_row = m_loc[0]
            for bb in range(1, bsz):
                m_loc_row = jnp.where(lane_batch == bb, m_loc[bb], m_loc_row)
            ks_t = kt_pair * jnp.exp2(w_row - m_loc_row)
            lhs = jnp.concatenate([jnp.where(lane_batch == bb, ks_t, 0.0) for bb in range(bsz)], axis=0)
            upd = _dot(lhs, v2)
            for bb in range(bsz):
                m_new = jnp.maximum(b_end[bb] + m_prev[bb], m_loc[bb])
                s_old = jnp.exp2(b_end[bb] + m_prev[bb] - m_new)
                s_loc = jnp.exp2(m_loc[bb] - m_new)
                mlstm_state[h, bb] = s_old * mlstm_state[h, bb] + s_loc * upd[bb * LANE:(bb + 1) * LANE]
                mlstm_m[bb:bb + 1, h:h + 1] = m_new
        y = _halves(lane, y_heads[0], y_heads[1])
        mu_h = _head_sums(y, lane) * (1.0 / HEAD_DIM)
        yc = y - mu_h
        var = _head_sums(yc * yc, lane) * (1.0 / HEAD_DIM)
        y = yc * lax.rsqrt(var + EPS) * mnorm_ref[:, p * LANE:(p + 1) * LANE]
        y = y * jax.nn.sigmoid(load(OFF["mout"] + p * LANE, LANE))
        store((GLA_W + HGRN_W) // LANE + p, y)


def _mixers(z, wgate, bgate, gnorm, lb, hnorm, conv_w, gate_b, mnorm):
    bsz, s, nz = z.shape
    d_out = GLA_W + HGRN_W + MLSTM_W
    n_chunks = s // CHUNK
    n_pieces = CHUNK // PIECE
    rows = bsz * CHUNK
    mask = jnp.asarray(_stack_masks(bsz))
    const = lambda c: (0, 0)
    out = pl.pallas_call(
        functools.partial(_mixer_kernel, bsz=bsz),
        grid=(n_chunks,),
        in_specs=[pl.BlockSpec((bsz, None, n_pieces, PIECE, nz), lambda c: (0, c, 0, 0, 0)),
                  pl.BlockSpec((rows, rows), const), pl.BlockSpec((rows, rows), const),
                  pl.BlockSpec(wgate.shape, const), pl.BlockSpec(bgate.shape, const),
                  pl.BlockSpec(gnorm.shape, const), pl.BlockSpec(lb.shape, const),
                  pl.BlockSpec(hnorm.shape, const), pl.BlockSpec(conv_w.shape, const),
                  pl.BlockSpec(gate_b.shape, const), pl.BlockSpec(mnorm.shape, const)],
        out_specs=pl.BlockSpec((bsz, None, n_pieces, PIECE, d_out), lambda c: (0, c, 0, 0, 0)),
        out_shape=jax.ShapeDtypeStruct((bsz, n_chunks, n_pieces, PIECE, d_out), BF16),
        scratch_shapes=[pltpu.VMEM((GLA_HEADS // 2, bsz, LANE, LANE), F32),
                        pltpu.VMEM((HGRN_HEADS // 2, bsz, LANE, LANE), F32),
                        pltpu.VMEM((MLSTM_HEADS, bsz, LANE, 2 * LANE), F32),
                        pltpu.VMEM((8, LANE), F32),
                        pltpu.VMEM((bsz, 8, 2 * MLSTM_W), F32)],
        compiler_params=pltpu.CompilerParams(dimension_semantics=("arbitrary",),
                                             vmem_limit_bytes=48 * 1024 * 1024),
        name="token_mixers",
    )(z.reshape(bsz, n_chunks, n_pieces, PIECE, nz), mask, mask.astype(BF16),
      wgate, bgate, gnorm, lb, hnorm, conv_w, gate_b, mnorm)
    return out.reshape(bsz, s, d_out)


def _out_mlp_kernel(mixed_ref, x_ref, gate1_ref, shift_ref, scale_ref, gate2_ref, gain_ref,
                    wout_ref, w1_ref, w2_ref, fnorm_ref, o_ref, *, final):
    x = x_ref[...] + gate1_ref[...] * jnp.dot(mixed_ref[...], wout_ref[...], preferred_element_type=F32)
    y = _rms(x) * gain_ref[...]
    y = y * (1.0 + scale_ref[...]) + shift_ref[...]
    u = jnp.dot(y.astype(BF16), w1_ref[...], preferred_element_type=F32)
    u = jnp.square(jnp.maximum(u, 0.0))
    x = x + gate2_ref[...] * jnp.dot(u.astype(BF16), w2_ref[...], preferred_element_type=F32)
    if final:
        x = _rms(x) * fnorm_ref[...]
    o_ref[...] = x


def _out_mlp(mixed, x, gate1, shift, scale, gate2, gain, w_out, w1, w2, fnorm, *, final, tm=512):
    bsz, s, d = x.shape
    dff = w1.shape[-1]
    tok = pl.BlockSpec((None, tm, d), lambda b, i: (b, i, 0))
    mod = pl.BlockSpec((None, 1, d), lambda b, i: (b, 0, 0))
    return pl.pallas_call(
        functools.partial(_out_mlp_kernel, final=final),
        grid=(bsz, s // tm),
        in_specs=[tok, tok, mod, mod, mod, mod, _resident((1, d)),
                  _resident((d, d)), _resident((d, dff)), _resident((dff, d)),
                  _resident((1, d))],
        out_specs=tok,
        out_shape=jax.ShapeDtypeStruct((bsz, s, d), F32),
        compiler_params=pltpu.CompilerParams(dimension_semantics=("arbitrary", "arbitrary"),
                                             vmem_limit_bytes=56 * 1024 * 1024),
        name="out_proj_mlp",
    )(mixed, x, gate1, shift, scale, gate2, gain, w_out, w1, w2, fnorm)


def kernel(x, c, w_ada, b_ada, norm_mix, norm_mlp, w_in, gla_w_gate, gla_b_gate, gla_norm, hgrn_lb_logits,
           hgrn_norm, mlstm_conv, mlstm_gate_b, mlstm_norm, w_out, w_ff1, w_ff2, final_norm):
    depth, d, _ = w_ada.shape
    bsz = x.shape[0]
    mod = _adaln(c.astype(F32), w_ada.astype(F32), b_ada.astype(F32))
    mod = mod.reshape(depth, bsz, 6, 1, d)
    lower = _lower_bounds(hgrn_lb_logits.astype(F32))
    w_in_p = _permute_in_proj(w_in)
    gate_b = jnp.concatenate([mlstm_gate_b[:, MLSTM_HEADS:], mlstm_gate_b[:, :MLSTM_HEADS],
                              jnp.zeros((depth, LANE - 2 * MLSTM_HEADS), F32)], axis=-1)
    wgate = jnp.zeros((depth, LANE, GLA_QK_PAD), F32)
    wgate = wgate.at[:, GLOW_LANE:GLOW_LANE + GLA_RANK, :GLA_QK].set(gla_w_gate).astype(BF16)
    bgate = jnp.pad(gla_b_gate, ((0, 0), (0, GLA_QK_PAD - GLA_QK)))
    gnorm2 = jnp.concatenate([gla_norm, gla_norm], axis=-1)
    hnorm2 = jnp.concatenate([hgrn_norm, hgrn_norm], axis=-1)
    w_out_b, w1_b, w2_b = w_out.astype(BF16), w_ff1.astype(BF16), w_ff2.astype(BF16)
    fnorm = final_norm.reshape(1, d).astype(F32)
    for l in range(depth):
        shift1, scale1, gate1, shift2, scale2, gate2 = (mod[l, :, j] for j in range(6))
        z = _in_proj(x, shift1, scale1, norm_mix[l].reshape(1, d), w_in_p[l])
        mixed = _mixers(z, wgate[l], bgate[l].reshape(1, -1), gnorm2[l].reshape(1, -1),
                        lower[l].reshape(1, -1), hnorm2[l].reshape(1, -1), mlstm_conv[l],
                        gate_b[l].reshape(1, -1), mlstm_norm[l].reshape(1, -1))
        x = _out_mlp(mixed, x, gate1, shift2, scale2, gate2, norm_mlp[l].reshape(1, d),
                     w_out_b[l], w1_b[l], w2_b[l], fnorm, final=(l == depth - 1))
    return x
```

```python
import functools

import numpy as np
import jax
import jax.numpy as jnp
from jax import lax
from jax.experimental import pallas as pl
from jax.experimental.pallas import tpu as pltpu

F32 = jnp.float32
BF16 = jnp.bfloat16

HEAD_DIM = 64
GLA_HEADS = 6
GLA_DK = 32
GLA_RANK = 16
GLA_TAU = 16.0
HGRN_HEADS = 6
HGRN_DK = 64
MLSTM_HEADS = 4
CONV_WIDTH = 4
EPS = 1e-6
CHUNK = 64
PIECE = 16
LANE = 128
NEG = -1e30
MAX_EXP2 = 115.0
LOG2E = 1.4426950408889634
MIB = 1024 * 1024

GLA_QK = GLA_HEADS * GLA_DK
GLA_W = GLA_HEADS * HEAD_DIM
HGRN_W = HGRN_HEADS * HEAD_DIM
MLSTM_W = MLSTM_HEADS * HEAD_DIM
GLA_QK_PAD = 2 * LANE

_REF_SEGMENTS = (("gq", GLA_QK), ("gk", GLA_QK), ("gv", GLA_W), ("glow", GLA_RANK), ("gout", GLA_W),
                 ("hq", HGRN_W), ("hf", HGRN_W), ("hi", HGRN_W), ("hout", HGRN_W),
                 ("mqk", 2 * MLSTM_W), ("mv", MLSTM_W), ("mi", MLSTM_HEADS), ("mf", MLSTM_HEADS),
                 ("mout", MLSTM_W))
_MY_ORDER = (("gq", 0), ("pad", GLA_QK_PAD - GLA_QK), ("gk", 0), ("pad", GLA_QK_PAD - GLA_QK),
             ("gv", 0), ("gout", 0), ("hq", 0), ("hf", 0), ("hi", 0), ("hout", 0),
             ("mqk", 0), ("mv", 0), ("mout", 0), ("mf", 0), ("mi", 0), ("glow", 0))


def _layout():
    ref_off, off = {}, 0
    for name, width in _REF_SEGMENTS:
        ref_off[name] = (off, width)
        off += width
    my_off, off = {}, 0
    for name, pad in _MY_ORDER:
        if name == "pad":
            off += pad
        else:
            my_off[name] = off
            off += ref_off[name][1]
    padded = -(-off // LANE) * LANE
    return ref_off, my_off, off, padded


_REF_OFF, OFF, _NZ_USED, NZ = _layout()
SMALL = OFF["mf"]
GLOW_LANE = OFF["glow"] - SMALL
assert SMALL % LANE == 0 and OFF["mi"] == SMALL + MLSTM_HEADS and GLOW_LANE == 2 * MLSTM_HEADS
assert all(OFF[n] % LANE == 0 for n in ("gq", "gk", "gv", "gout", "hq", "hf", "hi", "hout", "mqk", "mv", "mout"))


def _permute_kernel(w_ref, o_ref):
    off = 0
    for name, pad in _MY_ORDER:
        if name == "pad":
            o_ref[:, off:off + pad] = jnp.zeros((o_ref.shape[0], pad), o_ref.dtype)
            off += pad
        else:
            lo, width = _REF_OFF[name]
            o_ref[:, off:off + width] = w_ref[:, lo:lo + width].astype(o_ref.dtype)
            off += width
    o_ref[:, off:] = jnp.zeros((o_ref.shape[0], o_ref.shape[1] - off), o_ref.dtype)


def _permute_in_proj(w_in, *, tr=256):
    depth, d, d_in = w_in.shape
    return pl.pallas_call(
        _permute_kernel,
        grid=(depth, d // tr),
        in_specs=[pl.BlockSpec((None, tr, d_in), lambda l, i: (l, i, 0))],
        out_specs=pl.BlockSpec((None, tr, NZ), lambda l, i: (l, i, 0)),
        out_shape=jax.ShapeDtypeStruct((depth, d, NZ), BF16),
        name="permute_in_proj_weight",
    )(w_in)


def _silu(x):
    return x * jax.nn.sigmoid(x)


def _log2_sigmoid(x):
    return jnp.minimum(x, 0.0) * LOG2E - jnp.log2(1.0 + jnp.exp2(jnp.abs(x) * (-LOG2E)))


def _dot(a, b):
    return jnp.dot(a.astype(BF16), b.astype(BF16), preferred_element_type=F32)


def _dot_nt(a, b):
    return lax.dot_general(a.astype(BF16), b.astype(BF16), (((1,), (1,)), ((), ())),
                           preferred_element_type=F32)


def _cumsum_rows(g, tri):
    hi = g.astype(BF16)
    lo = (g - hi.astype(F32)).astype(BF16)
    return (jnp.dot(tri, hi, preferred_element_type=F32) + jnp.dot(tri, lo, preferred_element_type=F32))


def _adaln_kernel(c_ref, w_ref, b_ref, o_ref):
    cond = _silu(c_ref[...])
    o_ref[...] = jnp.dot(cond, w_ref[...], preferred_element_type=F32,
                         precision=lax.Precision.HIGHEST) + b_ref[...]


def _adaln(c, w_ada, b_ada, *, tn=2048):
    depth, d, n = w_ada.shape
    bsz = c.shape[0]
    return pl.pallas_call(
        _adaln_kernel,
        grid=(depth, n // tn),
        in_specs=[pl.BlockSpec((bsz, d), lambda l, j: (0, 0)),
                  pl.BlockSpec((None, d, tn), lambda l, j: (l, 0, j)),
                  pl.BlockSpec((None, 1, tn), lambda l, j: (l, 0, j))],
        out_specs=pl.BlockSpec((None, bsz, tn), lambda l, j: (l, 0, j)),
        out_shape=jax.ShapeDtypeStruct((depth, bsz, n), F32),
        name="adaln_modulation",
    )(c, w_ada, b_ada.reshape(depth, 1, n))


def _lower_bound_kernel(logit_ref, o_ref):
    x = logit_ref[...]
    e = jnp.exp(x - jnp.max(x, axis=0, keepdims=True))
    p = e / jnp.sum(e, axis=0, keepdims=True)
    depth = x.shape[0]
    acc = p[0:1]
    rows = [acc]
    for l in range(1, depth):
        acc = acc + p[l:l + 1]
        rows.append(acc)
    cum = jnp.concatenate(rows, axis=0)
    o_ref[...] = cum - cum[0:1]


def _lower_bounds(lb_logits):
    return pl.pallas_call(
        _lower_bound_kernel,
        out_shape=jax.ShapeDtypeStruct(lb_logits.shape, F32),
        name="hgrn_lower_bounds",
    )(lb_logits)


def _rms(x):
    return x * lax.rsqrt(jnp.mean(x * x, axis=-1, keepdims=True) + EPS)


def _layer_block(arr, layer):
    tail = arr.shape[1:]
    return pl.BlockSpec((None,) + tail, lambda *_: (layer,) + (0,) * len(tail), pipeline_mode=pl.Buffered(1))


def _in_proj_kernel(x_ref, shift_ref, scale_ref, gain_ref, w_ref, z_ref):
    y = _rms(x_ref[...]) * gain_ref[...]
    y = y * (1.0 + scale_ref[...]) + shift_ref[...]
    z_ref[...] = jnp.dot(y.astype(BF16), w_ref[...], preferred_element_type=F32)


def _in_proj(x, shift, scale, gains, w, layer, *, tm=512):
    bsz, s, d = x.shape
    nz = w.shape[-1]
    return pl.pallas_call(
        _in_proj_kernel,
        grid=(bsz, s // tm),
        in_specs=[pl.BlockSpec((None, tm, d), lambda b, i: (b, i, 0)),
                  pl.BlockSpec((None, 1, d), lambda b, i: (b, 0, 0)),
                  pl.BlockSpec((None, 1, d), lambda b, i: (b, 0, 0)),
                  _layer_block(gains, layer),
                  _layer_block(w, layer)],
        out_specs=pl.BlockSpec((None, tm, nz), lambda b, i: (b, i, 0)),
        out_shape=jax.ShapeDtypeStruct((bsz, s, nz), F32),
        compiler_params=pltpu.CompilerParams(dimension_semantics=("arbitrary", "arbitrary"),
                                             vmem_limit_bytes=48 * MIB),
        name="norm_in_proj",
    )(x, shift, scale, gains, w)


def _stack_masks(bsz):
    r = np.arange(bsz * CHUNK)
    batch = (r // PIECE) % bsz
    pos = (r // (PIECE * bsz)) * PIECE + r % PIECE
    return ((batch[:, None] == batch[None, :]) & (pos[None, :] <= pos[:, None])).astype(np.float32)


def _pieces(x, bsz):
    n_groups = x.shape[0] // (PIECE * bsz)
    return [[x[(i * bsz + b) * PIECE:(i * bsz + b + 1) * PIECE] for b in range(bsz)] for i in range(n_groups)]


def _per_batch_rows(rows, bsz, reps):
    slab = jnp.concatenate([jnp.broadcast_to(rows[b], (PIECE, rows[b].shape[-1])) for b in range(bsz)], axis=0)
    return jnp.concatenate([slab] * reps, axis=0) if reps > 1 else slab


def _per_batch_select(vals, batch_ids, bsz):
    out = vals[0]
    for b in range(1, bsz):
        out = jnp.where(batch_ids == b, vals[b], out)
    return out


def _halves(lane, even, odd):
    return jnp.where(lane < HEAD_DIM, even, odd)


def _head_sums(x, lane):
    lo = jnp.sum(jnp.where(lane < HEAD_DIM, x, 0.0), axis=-1, keepdims=True)
    hi = jnp.sum(jnp.where(lane < HEAD_DIM, 0.0, x), axis=-1, keepdims=True)
    return _halves(lane, lo, hi)


def _decayed_attention(q, k, g, v_tile, state_ref, mask_ref, tri, bsz, *, heads, dk, n_groups):
    rows, wk = q.shape
    glen = CHUNK // n_groups
    gp = glen // PIECE
    grows = rows // n_groups
    b = _cumsum_rows(g, tri)
    bp = _pieces(b, bsz)

    def ref_rows(gi):
        mid = gi * glen + glen // 2
        return _per_batch_rows([bp[mid // PIECE][bb][mid % PIECE:mid % PIECE + 1] for bb in range(bsz)], bsz, gp)

    refs = [ref_rows(gi) for gi in range(n_groups)]
    q_in = q * jnp.exp2(jnp.minimum(b - jnp.concatenate(refs, axis=0), MAX_EXP2))
    q_state = q * jnp.exp2(b)
    last = [bp[-1][bb][PIECE - 1:PIECE] for bb in range(bsz)]
    k_state = (k * jnp.exp2(_per_batch_rows(last, bsz, rows // (bsz * PIECE)) - b)).astype(BF16)
    yield
    k_in = []
    for gi in range(n_groups):
        end = (gi + 1) * grows
        ref = jnp.concatenate([refs[gi]] * (gi + 1), axis=0) if gi else refs[gi]
        k_in.append((k[:end] * jnp.exp2(jnp.minimum(ref - b[:end], MAX_EXP2))).astype(BF16))
    yield

    lane = lax.broadcasted_iota(jnp.int32, (1, LANE), 1)
    heads_per_tile = LANE // dk
    n_pairs = heads // 2
    probs = [None] * heads
    for kt in range(-(-heads // heads_per_tile)):
        tile_heads = list(range(kt * heads_per_tile, min(heads, (kt + 1) * heads_per_tile)))
        tl = slice(kt * LANE, (kt + 1) * LANE)
        per_head = {h: [] for h in tile_heads}
        for gi in range(n_groups):
            end = (gi + 1) * grows
            qg = q_in[gi * grows:end, tl]
            lhs = jnp.concatenate(
                [jnp.where((lane >= (h % heads_per_tile) * dk) & (lane < (h % heads_per_tile + 1) * dk), qg, 0.0)
                 for h in tile_heads], axis=0).astype(BF16)
            sc = _dot_nt(lhs, k_in[gi][:, tl])
            keep = mask_ref[gi * grows:end, :end]
            for n, h in enumerate(tile_heads):
                blk = (sc[n * grows:(n + 1) * grows] * keep).astype(BF16)
                if end < rows:
                    blk = jnp.concatenate([blk, jnp.zeros((grows, rows - end), BF16)], axis=1)
                per_head[h].append(blk)
        for h in tile_heads:
            probs[h] = jnp.concatenate(per_head[h], axis=0)
        yield

    lane_batch = (lax.broadcasted_iota(jnp.int32, (1, rows), 1) // PIECE) % bsz
    row_i = lax.broadcasted_iota(jnp.int32, (LANE, LANE), 0)
    col_i = lax.broadcasted_iota(jnp.int32, (LANE, LANE), 1)
    outs = []
    for p in range(n_pairs):
        kt = (2 * p) // heads_per_tile
        tl = slice(kt * LANE, (kt + 1) * LANE)
        v = v_tile(p)
        vb = v.astype(BF16)
        intra = _halves(lane, _dot(probs[2 * p], vb), _dot(probs[2 * p + 1], vb))
        qs = _pieces(q_state[:, tl], bsz)
        inter_b = []
        for bb in range(bsz):
            lhs = jnp.concatenate([qs[i][bb] for i in range(len(qs))], axis=0)
            inter_b.append(_dot_nt(lhs, state_ref[p, bb]))
        inter = jnp.concatenate([inter_b[bb][i * PIECE:(i + 1) * PIECE]
                                 for i in range(len(qs)) for bb in range(bsz)], axis=0)
        outs.append(intra + inter)
        yield
        vt = v.T
        lhs = jnp.concatenate([jnp.where(lane_batch == bb, vt, 0.0) for bb in range(bsz)], axis=0)
        upd = _dot(lhs, k_state[:, tl])
        same_head = (2 * p + row_i // HEAD_DIM) == (kt * heads_per_tile + col_i // dk)
        for bb in range(bsz):
            decay = jnp.exp2(last[bb][:, tl])
            state_ref[p, bb] = state_ref[p, bb] * decay + jnp.where(same_head, upd[bb * LANE:(bb + 1) * LANE], 0.0)
        yield
    return outs


def _mixer_kernel(z_ref, mask_ref, negmask_ref, tri_ref, wgate_ref, bgate_ref, gnorm_ref, lb_ref, hnorm_ref,
                  conv_ref, gateb_ref, mnorm_ref, o_ref, gla_state, hgrn_state, mlstm_state, mlstm_m, conv_carry,
                  *, bsz):
    @pl.when(pl.program_id(0) == 0)
    def _():
        gla_state[...] = jnp.zeros_like(gla_state)
        hgrn_state[...] = jnp.zeros_like(hgrn_state)
        mlstm_state[...] = jnp.zeros_like(mlstm_state)
        mlstm_m[...] = jnp.zeros_like(mlstm_m)
        conv_carry[...] = jnp.zeros_like(conv_carry)

    n_pieces = CHUNK // PIECE
    rows = bsz * CHUNK
    tri = tri_ref[...]
    lane = lax.broadcasted_iota(jnp.int32, (1, LANE), 1)

    def load(off, width):
        return jnp.concatenate([z_ref[b, i, :, off:off + width] for i in range(n_pieces) for b in range(bsz)], axis=0)

    def store(tile_idx, val):
        val = val.astype(o_ref.dtype)
        for i in range(n_pieces):
            for b in range(bsz):
                r0 = (i * bsz + b) * PIECE
                o_ref[b, i, :, tile_idx * LANE:(tile_idx + 1) * LANE] = val[r0:r0 + PIECE]

    small = load(SMALL, LANE)

    def gla_group():
        logits = _dot(small, wgate_ref[...]) + bgate_ref[...]
        g = _log2_sigmoid(logits) * (1.0 / GLA_TAU)
        q = load(OFF["gq"], GLA_QK_PAD) * (GLA_DK ** -0.5)
        k = load(OFF["gk"], GLA_QK_PAD)
        outs = yield from _decayed_attention(q, k, g, lambda p: load(OFF["gv"] + p * LANE, LANE), gla_state,
                                             mask_ref, tri, bsz, heads=GLA_HEADS, dk=GLA_DK, n_groups=4)
        for p in range(GLA_HEADS // 2):
            o = outs[p]
            o = o * lax.rsqrt(_head_sums(o * o, lane) * (1.0 / HEAD_DIM) + EPS) * gnorm_ref[...]
            o = o * _silu(load(OFF["gout"] + p * LANE, LANE))
            store(p, o)
            yield

    def hgrn_group():
        lb = lb_ref[...]
        f = load(OFF["hf"], HGRN_W)
        t = jnp.exp2(jnp.abs(f) * (-LOG2E))
        u = 1.0 + t
        log_lb = jnp.log2(lb)
        log_rest = jnp.log2(1.0 - lb) + (jnp.minimum(f, 0.0) * LOG2E - jnp.log2(u))
        g = jnp.maximum(log_lb, log_rest) + jnp.log2(1.0 + jnp.exp2(-jnp.abs(log_lb - log_rest)))
        k = (1.0 - lb) * (jnp.where(f > 0.0, t, 1.0) / u)
        q = _silu(load(OFF["hq"], HGRN_W))
        yield
        outs = yield from _decayed_attention(q, k, g, lambda p: load(OFF["hi"] + p * LANE, LANE), hgrn_state,
                                             mask_ref, tri, bsz, heads=HGRN_HEADS, dk=HGRN_DK, n_groups=4)
        for p in range(HGRN_HEADS // 2):
            o = outs[p]
            o = o * lax.rsqrt(_head_sums(o * o, lane) * (1.0 / HEAD_DIM) + EPS) * hnorm_ref[...]
            o = o * jax.nn.sigmoid(load(OFF["hout"] + p * LANE, LANE))
            store(GLA_W // LANE + p, o)
            yield

    def mlstm_group():
        conv_b = []
        for b in range(bsz):
            x_qk = jnp.concatenate([z_ref[b, i, :, OFF["mqk"]:OFF["mqk"] + 2 * MLSTM_W] for i in range(n_pieces)],
                                   axis=0)
            hist = jnp.concatenate([conv_carry[b], x_qk], axis=0)
            conv_carry[b] = x_qk[CHUNK - 8:CHUNK, :]
            lo = 8 - (CONV_WIDTH - 1)
            acc = conv_ref[0:1, :] * hist[lo:lo + CHUNK, :]
            for tap in range(1, CONV_WIDTH):
                acc = acc + conv_ref[tap:tap + 1, :] * hist[lo + tap:lo + tap + CHUNK, :]
            conv_b.append(acc)
            if b % 2:
                yield
        qk = _silu(jnp.concatenate([conv_b[b][i * PIECE:(i + 1) * PIECE]
                                    for i in range(n_pieces) for b in range(bsz)], axis=0))
        pre = small + gateb_ref[...]
        gates = jnp.where(lane < MLSTM_HEADS, _log2_sigmoid(pre),
                          jnp.where(lane < 2 * MLSTM_HEADS, pre * LOG2E, 0.0))
        cum = _cumsum_rows(gates, tri)
        row_form = jnp.where(lane < MLSTM_HEADS, cum, gates).T
        lane_batch = (lax.broadcasted_iota(jnp.int32, (1, rows), 1) // PIECE) % bsz
        row_batch = (lax.broadcasted_iota(jnp.int32, (rows, 1), 0) // PIECE) % bsz
        ones_tile = jnp.ones((rows, LANE), BF16)
        yield
        for p in range(MLSTM_HEADS // 2):
            q_pair = qk[:, p * LANE:(p + 1) * LANE]
            k_pair = qk[:, MLSTM_W + p * LANE:MLSTM_W + (p + 1) * LANE] * (HEAD_DIM ** -0.5)
            kt_pair = k_pair.T
            v2 = jnp.concatenate([load(OFF["mv"] + p * LANE, LANE).astype(BF16), ones_tile], axis=1)
            y_heads = []
            for hh in range(2):
                h = 2 * p + hh
                own = (lane >= hh * HEAD_DIM) & (lane < (hh + 1) * HEAD_DIM)
                b_row = row_form[h:h + 1, :]
                c_row = row_form[MLSTM_HEADS + h:MLSTM_HEADS + h + 1, :] - b_row
                b_col = jnp.sum(jnp.where(lane == h, cum, 0.0), axis=-1, keepdims=True)
                m_prev = [mlstm_m[bb:bb + 1, h:h + 1] for bb in range(bsz)]
                m_prev_col = _per_batch_select(m_prev, row_batch, bsz)
                c_max = jnp.max(c_row + negmask_ref[...], axis=-1, keepdims=True)
                mu = jnp.maximum(m_prev_col, c_max)
                q_own = jnp.where(own, q_pair, 0.0)
                prob = jnp.exp2((c_row - mu) + negmask_ref[...]) * _dot_nt(q_own, k_pair)
                res = _dot(prob, v2)
                qo = _pieces(q_own, bsz)
                inter_b = []
                for bb in range(bsz):
                    lhs = jnp.concatenate([qo[i][bb] for i in range(n_pieces)], axis=0)
                    inter_b.append(_dot(lhs, mlstm_state[h, bb]))
                inter = jnp.concatenate([inter_b[bb][i * PIECE:(i + 1) * PIECE]
                                         for i in range(n_pieces) for bb in range(bsz)], axis=0)
                res = res + jnp.exp2(m_prev_col - mu) * inter
                den = jnp.maximum(jnp.abs(res[:, LANE:]), jnp.exp2(-(b_col + mu)))
                y_heads.append(res[:, :LANE] / den)
                yield
                b_end = [b_col[rows - (bsz - bb) * PIECE + PIECE - 1:rows - (bsz - bb) * PIECE + PIECE, :]
                         for bb in range(bsz)]
                w_row = _per_batch_select(b_end, lane_batch, bsz) + c_row
                m_loc = [jnp.max(jnp.where(lane_batch == bb, w_row, NEG), axis=-1, keepdims=True)
                         for bb in range(bsz)]
                m_new = [jnp.maximum(b_end[bb] + m_prev[bb], m_loc[bb]) for bb in range(bsz)]
                ks_t = kt_pair * jnp.exp2(w_row - _per_batch_select(m_new, lane_batch, bsz))
                lhs = jnp.concatenate([jnp.where(lane_batch == bb, ks_t, 0.0) for bb in range(bsz)], axis=0)
                upd = _dot(lhs, v2)
                for bb in range(bsz):
                    s_old = jnp.exp2(b_end[bb] + m_prev[bb] - m_new[bb])
                    mlstm_state[h, bb] = s_old * mlstm_state[h, bb] + upd[bb * LANE:(bb + 1) * LANE]
                    mlstm_m[bb:bb + 1, h:h + 1] = m_new[bb]
                yield
            y = _halves(lane, y_heads[0], y_heads[1])
            mu_h = _head_sums(y, lane) * (1.0 / HEAD_DIM)
            yc = y - mu_h
            var = _head_sums(yc * yc, lane) * (1.0 / HEAD_DIM)
            y = yc * lax.rsqrt(var + EPS) * mnorm_ref[:, p * LANE:(p + 1) * LANE]
            y = y * jax.nn.sigmoid(load(OFF["mout"] + p * LANE, LANE))
            store((GLA_W + HGRN_W) // LANE + p, y)
            yield

    pending = [gla_group(), hgrn_group(), mlstm_group()]
    while pending:
        for gen in list(pending):
            try:
                next(gen)
            except StopIteration:
                pending.remove(gen)


def _mixers(z, params, layer):
    bsz, s, nz = z.shape
    d_out = GLA_W + HGRN_W + MLSTM_W
    n_chunks = s // CHUNK
    n_pieces = CHUNK // PIECE
    rows = bsz * CHUNK
    mask = _stack_masks(bsz)
    consts = [jnp.asarray(mask), jnp.asarray((1.0 - mask) * NEG), jnp.asarray(mask, dtype=BF16)]
    whole = lambda a: pl.BlockSpec(a.shape, lambda c: (0,) * a.ndim, pipeline_mode=pl.Buffered(1))
    out = pl.pallas_call(
        functools.partial(_mixer_kernel, bsz=bsz),
        grid=(n_chunks,),
        in_specs=([pl.BlockSpec((bsz, None, n_pieces, PIECE, nz), lambda c: (0, c, 0, 0, 0))]
                  + [whole(a) for a in consts] + [_layer_block(a, layer) for a in params]),
        out_specs=pl.BlockSpec((bsz, None, n_pieces, PIECE, d_out), lambda c: (0, c, 0, 0, 0)),
        out_shape=jax.ShapeDtypeStruct((bsz, n_chunks, n_pieces, PIECE, d_out), BF16),
        scratch_shapes=[pltpu.VMEM((GLA_HEADS // 2, bsz, LANE, LANE), F32),
                        pltpu.VMEM((HGRN_HEADS // 2, bsz, LANE, LANE), F32),
                        pltpu.VMEM((MLSTM_HEADS, bsz, LANE, 2 * LANE), F32),
                        pltpu.VMEM((8, LANE), F32),
                        pltpu.VMEM((bsz, 8, 2 * MLSTM_W), F32)],
        compiler_params=pltpu.CompilerParams(dimension_semantics=("arbitrary",),
                                             vmem_limit_bytes=48 * MIB),
        name="token_mixers",
    )(z.reshape(bsz, n_chunks, n_pieces, PIECE, nz), *consts, *params)
    return out.reshape(bsz, s, d_out)


def _out_mlp_kernel(mixed_ref, x_ref, gate1_ref, shift_ref, scale_ref, gate2_ref, gain_ref,
                    wout_ref, w1_ref, w2_ref, fnorm_ref, o_ref, *, final):
    x = x_ref[...] + gate1_ref[...] * jnp.dot(mixed_ref[...], wout_ref[...], preferred_element_type=F32)
    y = _rms(x) * gain_ref[...]
    y = y * (1.0 + scale_ref[...]) + shift_ref[...]
    u = jnp.dot(y.astype(BF16), w1_ref[...], preferred_element_type=F32)
    u = jnp.square(jnp.maximum(u, 0.0))
    x = x + gate2_ref[...] * jnp.dot(u.astype(BF16), w2_ref[...], preferred_element_type=F32)
    if final:
        x = _rms(x) * fnorm_ref[...]
    o_ref[...] = x


def _out_mlp(mixed, x, gate1, shift, scale, gate2, gains, w_out, w1, w2, fnorm, layer, *, final, tm=512):
    bsz, s, d = x.shape
    tok = pl.BlockSpec((None, tm, d), lambda b, i: (b, i, 0))
    mod = pl.BlockSpec((None, 1, d), lambda b, i: (b, 0, 0))
    return pl.pallas_call(
        functools.partial(_out_mlp_kernel, final=final),
        grid=(bsz, s // tm),
        in_specs=[tok, tok, mod, mod, mod, mod, _layer_block(gains, layer),
                  _layer_block(w_out, layer), _layer_block(w1, layer), _layer_block(w2, layer),
                  pl.BlockSpec((1, d), lambda b, i: (0, 0), pipeline_mode=pl.Buffered(1))],
        out_specs=tok,
        out_shape=jax.ShapeDtypeStruct((bsz, s, d), F32),
        compiler_params=pltpu.CompilerParams(dimension_semantics=("arbitrary", "arbitrary"),
                                             vmem_limit_bytes=56 * MIB),
        name="out_proj_mlp",
    )(mixed, x, gate1, shift, scale, gate2, gains, w_out, w1, w2, fnorm)


def kernel(x, c, w_ada, b_ada, norm_mix, norm_mlp, w_in, gla_w_gate, gla_b_gate, gla_norm, hgrn_lb_logits,
           hgrn_norm, mlstm_conv, mlstm_gate_b, mlstm_norm, w_out, w_ff1, w_ff2, final_norm):
    depth, d, _ = w_ada.shape
    bsz = x.shape[0]
    mod = _adaln(c.astype(F32), w_ada.astype(F32), b_ada.astype(F32))
    mod = mod.reshape(depth, bsz, 6, 1, d)
    lower = _lower_bounds(hgrn_lb_logits.astype(F32))
    w_in_p = _permute_in_proj(w_in)
    gate_b = jnp.concatenate([mlstm_gate_b[:, MLSTM_HEADS:], mlstm_gate_b[:, :MLSTM_HEADS],
                              jnp.zeros((depth, LANE - 2 * MLSTM_HEADS), F32)], axis=-1)
    wgate = jnp.zeros((depth, LANE, GLA_QK_PAD), F32)
    wgate = wgate.at[:, GLOW_LANE:GLOW_LANE + GLA_RANK, :GLA_QK].set(gla_w_gate).astype(BF16)
    row = lambda a: a.reshape(depth, 1, -1).astype(F32)
    mixer_params = [wgate, row(jnp.pad(gla_b_gate, ((0, 0), (0, GLA_QK_PAD - GLA_QK)))),
                    row(jnp.concatenate([gla_norm, gla_norm], axis=-1)),
                    row(lower), row(jnp.concatenate([hgrn_norm, hgrn_norm], axis=-1)),
                    mlstm_conv.astype(F32), row(gate_b), row(mlstm_norm)]
    w_out_b, w1_b, w2_b = w_out.astype(BF16), w_ff1.astype(BF16), w_ff2.astype(BF16)
    gains_mix, gains_mlp = row(norm_mix), row(norm_mlp)
    fnorm = final_norm.reshape(1, d).astype(F32)
    for l in range(depth):
        shift1, scale1, gate1, shift2, scale2, gate2 = (mod[l, :, j] for j in range(6))
        z = _in_proj(x, shift1, scale1, gains_mix, w_in_p, l)
        mixed = _mixers(z, mixer_params, l)
        x = _out_mlp(mixed, x, gate1, shift2, scale2, gate2, gains_mlp, w_out_b, w1_b, w2_b, fnorm, l,
                     final=(l == depth - 1))
    return x
```

```python
import functools

import numpy as np
import jax
import jax.numpy as jnp
from jax import lax
from jax.experimental import pallas as pl
from jax.experimental.pallas import tpu as pltpu

F32 = jnp.float32
BF16 = jnp.bfloat16

HEAD_DIM = 64
GLA_HEADS = 6
GLA_DK = 32
GLA_RANK = 16
GLA_TAU = 16.0
HGRN_HEADS = 6
HGRN_DK = 64
MLSTM_HEADS = 4
CONV_WIDTH = 4
EPS = 1e-6
CHUNK = 64
CHUNKS_PER_STEP = 2
PIECE = 16
LANE = 128
NEG = -1e30
MAX_EXP2 = 115.0
LOG2E = 1.4426950408889634
MIB = 1024 * 1024

GLA_QK = GLA_HEADS * GLA_DK
GLA_W = GLA_HEADS * HEAD_DIM
HGRN_W = HGRN_HEADS * HEAD_DIM
MLSTM_W = MLSTM_HEADS * HEAD_DIM
GLA_QK_PAD = 2 * LANE

_REF_SEGMENTS = (("gq", GLA_QK), ("gk", GLA_QK), ("gv", GLA_W), ("glow", GLA_RANK), ("gout", GLA_W),
                 ("hq", HGRN_W), ("hf", HGRN_W), ("hi", HGRN_W), ("hout", HGRN_W),
                 ("mqk", 2 * MLSTM_W), ("mv", MLSTM_W), ("mi", MLSTM_HEADS), ("mf", MLSTM_HEADS),
                 ("mout", MLSTM_W))
_MY_ORDER = (("gq", 0), ("pad", GLA_QK_PAD - GLA_QK), ("gk", 0), ("pad", GLA_QK_PAD - GLA_QK),
             ("gv", 0), ("gout", 0), ("hq", 0), ("hf", 0), ("hi", 0), ("hout", 0),
             ("mqk", 0), ("mv", 0), ("mout", 0), ("mf", 0), ("mi", 0), ("glow", 0))


def _layout():
    ref_off, off = {}, 0
    for name, width in _REF_SEGMENTS:
        ref_off[name] = (off, width)
        off += width
    my_off, off = {}, 0
    for name, pad in _MY_ORDER:
        if name == "pad":
            off += pad
        else:
            my_off[name] = off
            off += ref_off[name][1]
    padded = -(-off // LANE) * LANE
    return ref_off, my_off, off, padded


_REF_OFF, OFF, _NZ_USED, NZ = _layout()
SMALL = OFF["mf"]
GLOW_LANE = OFF["glow"] - SMALL
assert SMALL % LANE == 0 and OFF["mi"] == SMALL + MLSTM_HEADS and GLOW_LANE == 2 * MLSTM_HEADS
assert all(OFF[n] % LANE == 0 for n in ("gq", "gk", "gv", "gout", "hq", "hf", "hi", "hout", "mqk", "mv", "mout"))


def _permute_kernel(wt_ref, o_ref, rows_scr):
    off = 0
    for name, pad in _MY_ORDER:
        if name == "pad":
            rows_scr[off:off + pad, :] = jnp.zeros((pad, rows_scr.shape[1]), F32)
            off += pad
        else:
            lo, width = _REF_OFF[name]
            rows_scr[off:off + width, :] = wt_ref[lo:lo + width, :]
            off += width
    rows_scr[off:, :] = jnp.zeros((rows_scr.shape[0] - off, rows_scr.shape[1]), F32)
    for t in range(rows_scr.shape[0] // LANE):
        o_ref[:, t * LANE:(t + 1) * LANE] = rows_scr[t * LANE:(t + 1) * LANE, :].T.astype(o_ref.dtype)


def _permute_in_proj(w_in, *, tc=256):
    depth, d, d_in = w_in.shape
    return pl.pallas_call(
        _permute_kernel,
        grid=(depth, d // tc),
        in_specs=[pl.BlockSpec((None, d_in, tc), lambda l, i: (l, 0, i))],
        out_specs=pl.BlockSpec((None, tc, NZ), lambda l, i: (l, i, 0)),
        out_shape=jax.ShapeDtypeStruct((depth, d, NZ), BF16),
        scratch_shapes=[pltpu.VMEM((NZ, tc), F32)],
        name="permute_in_proj_weight",
    )(jnp.swapaxes(w_in, 1, 2))


def _silu(x):
    return x * jax.nn.sigmoid(x)


def _log2_sigmoid(x):
    return jnp.minimum(x, 0.0) * LOG2E - jnp.log2(1.0 + jnp.exp2(jnp.abs(x) * (-LOG2E)))


def _dot(a, b):
    return jnp.dot(a.astype(BF16), b.astype(BF16), preferred_element_type=F32)


def _dot_nt(a, b):
    return lax.dot_general(a.astype(BF16), b.astype(BF16), (((1,), (1,)), ((), ())),
                           preferred_element_type=F32)


def _cumsum_rows(g, tri):
    hi = g.astype(BF16)
    lo = (g - hi.astype(F32)).astype(BF16)
    return (jnp.dot(tri, hi, preferred_element_type=F32) + jnp.dot(tri, lo, preferred_element_type=F32))


def _adaln_kernel(c_ref, w_ref, b_ref, o_ref):
    cond = _silu(c_ref[...])
    o_ref[...] = _dot(cond, w_ref[...]) + b_ref[...]


def _adaln(c, w_ada, b_ada, *, tn=2048):
    depth, d, n = w_ada.shape
    bsz = c.shape[0]
    return pl.pallas_call(
        _adaln_kernel,
        grid=(depth, n // tn),
        in_specs=[pl.BlockSpec((bsz, d), lambda l, j: (0, 0)),
                  pl.BlockSpec((None, d, tn), lambda l, j: (l, 0, j)),
                  pl.BlockSpec((None, 1, tn), lambda l, j: (l, 0, j))],
        out_specs=pl.BlockSpec((None, bsz, tn), lambda l, j: (l, 0, j)),
        out_shape=jax.ShapeDtypeStruct((depth, bsz, n), F32),
        name="adaln_modulation",
    )(c, w_ada, b_ada.reshape(depth, 1, n))


def _lower_bound_kernel(logit_ref, o_ref):
    x = logit_ref[...]
    e = jnp.exp(x - jnp.max(x, axis=0, keepdims=True))
    p = e / jnp.sum(e, axis=0, keepdims=True)
    depth = x.shape[0]
    acc = p[0:1]
    rows = [acc]
    for l in range(1, depth):
        acc = acc + p[l:l + 1]
        rows.append(acc)
    cum = jnp.concatenate(rows, axis=0)
    o_ref[...] = cum - cum[0:1]


def _lower_bounds(lb_logits):
    return pl.pallas_call(
        _lower_bound_kernel,
        out_shape=jax.ShapeDtypeStruct(lb_logits.shape, F32),
        name="hgrn_lower_bounds",
    )(lb_logits)


def _rms(x):
    return x * lax.rsqrt(jnp.mean(x * x, axis=-1, keepdims=True) + EPS)


def _layer_block(arr, layer):
    tail = arr.shape[1:]
    return pl.BlockSpec((None,) + tail, lambda *_: (layer,) + (0,) * len(tail), pipeline_mode=pl.Buffered(1))


def _in_proj_kernel(x_ref, shift_ref, scale_ref, gain_ref, w_ref, z_ref):
    y = _rms(x_ref[...]) * gain_ref[...]
    y = y * (1.0 + scale_ref[...]) + shift_ref[...]
    z_ref[...] = jnp.dot(y.astype(BF16), w_ref[...], preferred_element_type=F32)


def _in_proj(x, shift, scale, gains, w, layer, *, tm=512):
    bsz, s, d = x.shape
    nz = w.shape[-1]
    return pl.pallas_call(
        _in_proj_kernel,
        grid=(bsz, s // tm),
        in_specs=[pl.BlockSpec((None, tm, d), lambda b, i: (b, i, 0)),
                  pl.BlockSpec((None, 1, d), lambda b, i: (b, 0, 0)),
                  pl.BlockSpec((None, 1, d), lambda b, i: (b, 0, 0)),
                  _layer_block(gains, layer),
                  _layer_block(w, layer)],
        out_specs=pl.BlockSpec((None, tm, nz), lambda b, i: (b, i, 0)),
        out_shape=jax.ShapeDtypeStruct((bsz, s, nz), F32),
        compiler_params=pltpu.CompilerParams(dimension_semantics=("arbitrary", "arbitrary"),
                                             vmem_limit_bytes=48 * MIB),
        name="norm_in_proj",
    )(x, shift, scale, gains, w)


def _stack_masks(bsz):
    r = np.arange(bsz * CHUNK)
    batch = (r // PIECE) % bsz
    pos = (r // (PIECE * bsz)) * PIECE + r % PIECE
    return ((batch[:, None] == batch[None, :]) & (pos[None, :] <= pos[:, None])).astype(np.float32)


def _shift_matrix(bsz):
    rows = bsz * CHUNK
    r = np.arange(rows)
    batch = (r // PIECE) % bsz
    pos = (r // (PIECE * bsz)) * PIECE + r % PIECE
    same = batch[:, None] == batch[None, :]
    blocks = []
    for d in range(CONV_WIDTH - 1, 0, -1):
        cur = same & (pos[None, :] == pos[:, None] - d)
        prev = same & (pos[None, :] == pos[:, None] - d + CHUNK)
        blocks.append(np.concatenate([cur, prev], axis=1))
    return np.concatenate(blocks, axis=0).astype(np.float32)


def _pieces(x, bsz):
    n_groups = x.shape[0] // (PIECE * bsz)
    return [[x[(i * bsz + b) * PIECE:(i * bsz + b + 1) * PIECE] for b in range(bsz)] for i in range(n_groups)]


def _per_batch_rows(rows, bsz, reps):
    slab = jnp.concatenate([jnp.broadcast_to(rows[b], (PIECE, rows[b].shape[-1])) for b in range(bsz)], axis=0)
    return jnp.concatenate([slab] * reps, axis=0) if reps > 1 else slab


def _per_batch_select(vals, batch_ids, bsz):
    out = vals[0]
    for b in range(1, bsz):
        out = jnp.where(batch_ids == b, vals[b], out)
    return out


def _halves(lane, even, odd):
    return jnp.where(lane < HEAD_DIM, even, odd)


def _head_sums(x, lane):
    lo = jnp.sum(jnp.where(lane < HEAD_DIM, x, 0.0), axis=-1, keepdims=True)
    hi = jnp.sum(jnp.where(lane < HEAD_DIM, 0.0, x), axis=-1, keepdims=True)
    return _halves(lane, lo, hi)


class _Carry:
    def __init__(self, refs, shared, last):
        self.refs, self.shared, self.last = refs, shared, last
        self.out = {}

    def read(self, name, idx):
        if self.shared is None:
            return self.refs[name][idx]
        while (name, idx) not in self.shared:
            yield
        return self.shared[(name, idx)]

    def write(self, name, idx, val):
        if self.last:
            self.refs[name][idx] = val
        else:
            self.out[(name, idx)] = val


def _decayed_attention(q, k, g, v_tile, carry, name, mask_ref, tri, bsz, *, heads, dk, n_groups):
    rows, wk = q.shape
    glen = CHUNK // n_groups
    gp = glen // PIECE
    grows = rows // n_groups
    b = _cumsum_rows(g, tri)
    bp = _pieces(b, bsz)

    def ref_rows(gi):
        mid = gi * glen + glen // 2
        return _per_batch_rows([bp[mid // PIECE][bb][mid % PIECE:mid % PIECE + 1] for bb in range(bsz)], bsz, gp)

    refs = [ref_rows(gi) for gi in range(n_groups)]
    q_in = q * jnp.exp2(jnp.minimum(b - jnp.concatenate(refs, axis=0), MAX_EXP2))
    q_state = q * jnp.exp2(b)
    last = [bp[-1][bb][PIECE - 1:PIECE] for bb in range(bsz)]
    k_state = (k * jnp.exp2(_per_batch_rows(last, bsz, rows // (bsz * PIECE)) - b)).astype(BF16)
    yield
    k_in = []
    for gi in range(n_groups):
        end = (gi + 1) * grows
        ref = jnp.concatenate([refs[gi]] * (gi + 1), axis=0) if gi else refs[gi]
        k_in.append((k[:end] * jnp.exp2(jnp.minimum(ref - b[:end], MAX_EXP2))).astype(BF16))
    yield

    lane = lax.broadcasted_iota(jnp.int32, (1, LANE), 1)
    heads_per_tile = LANE // dk
    n_pairs = heads // 2
    probs = [None] * heads
    for kt in range(-(-heads // heads_per_tile)):
        tile_heads = list(range(kt * heads_per_tile, min(heads, (kt + 1) * heads_per_tile)))
        tl = slice(kt * LANE, (kt + 1) * LANE)
        per_head = {h: [] for h in tile_heads}
        for gi in range(n_groups):
            end = (gi + 1) * grows
            qg = q_in[gi * grows:end, tl]
            lhs = jnp.concatenate(
                [jnp.where((lane >= (h % heads_per_tile) * dk) & (lane < (h % heads_per_tile + 1) * dk), qg, 0.0)
                 for h in tile_heads], axis=0).astype(BF16)
            sc = _dot_nt(lhs, k_in[gi][:, tl])
            keep = mask_ref[gi * grows:end, :end]
            for n, h in enumerate(tile_heads):
                blk = (sc[n * grows:(n + 1) * grows] * keep).astype(BF16)
                if end < rows:
                    blk = jnp.concatenate([blk, jnp.zeros((grows, rows - end), BF16)], axis=1)
                per_head[h].append(blk)
        for h in tile_heads:
            probs[h] = jnp.concatenate(per_head[h], axis=0)
        yield

    lane_batch = (lax.broadcasted_iota(jnp.int32, (1, rows), 1) // PIECE) % bsz
    row_i = lax.broadcasted_iota(jnp.int32, (LANE, LANE), 0)
    col_i = lax.broadcasted_iota(jnp.int32, (LANE, LANE), 1)
    outs = []
    for p in range(n_pairs):
        kt = (2 * p) // heads_per_tile
        tl = slice(kt * LANE, (kt + 1) * LANE)
        v = v_tile(p)
        vb = v.astype(BF16)
        intra = _halves(lane, _dot(probs[2 * p], vb), _dot(probs[2 * p + 1], vb))
        qs = _pieces(q_state[:, tl], bsz)
        inter_b, states = [], []
        for bb in range(bsz):
            lhs = jnp.concatenate([qs[i][bb] for i in range(len(qs))], axis=0)
            states.append((yield from carry.read(name, (p, bb))))
            inter_b.append(_dot_nt(lhs, states[bb]))
        inter = jnp.concatenate([inter_b[bb][i * PIECE:(i + 1) * PIECE]
                                 for i in range(len(qs)) for bb in range(bsz)], axis=0)
        outs.append(intra + inter)
        yield
        vt = v.T.astype(BF16)
        lhs = jnp.concatenate([jnp.where(lane_batch == bb, vt, jnp.zeros_like(vt)) for bb in range(bsz)], axis=0)
        upd = _dot(lhs, k_state[:, tl])
        same_head = (2 * p + row_i // HEAD_DIM) == (kt * heads_per_tile + col_i // dk)
        for bb in range(bsz):
            decay = jnp.exp2(last[bb][:, tl])
            carry.write(name, (p, bb),
                        states[bb] * decay + jnp.where(same_head, upd[bb * LANE:(bb + 1) * LANE], 0.0))
        yield
    return outs


def _mixer_kernel(z_ref, mask_ref, negmask_ref, tri_ref, shift_ref, wgate_ref, bgate_ref, gnorm_ref, lb_ref, hnorm_ref,
                  conv_ref, gateb_ref, mnorm_ref, o_ref, gla_state, hgrn_state, mlstm_state, mlstm_m, conv_carry,
                  *, bsz):
    @pl.when(pl.program_id(0) == 0)
    def _():
        gla_state[...] = jnp.zeros_like(gla_state)
        hgrn_state[...] = jnp.zeros_like(hgrn_state)
        mlstm_state[...] = jnp.zeros_like(mlstm_state)
        mlstm_m[...] = jnp.zeros_like(mlstm_m)
        conv_carry[...] = jnp.zeros_like(conv_carry)

    n_pieces = CHUNK // PIECE
    rows = bsz * CHUNK
    tri = tri_ref[...]
    lane = lax.broadcasted_iota(jnp.int32, (1, LANE), 1)
    refs = {"gla": gla_state, "hgrn": hgrn_state, "mlstm": mlstm_state, "m": mlstm_m, "conv": conv_carry}
    everything = (Ellipsis,)

    pending, shared = [], None
    for ci in range(CHUNKS_PER_STEP):
        carry = _Carry(refs, shared, last=(ci == CHUNKS_PER_STEP - 1))
        pending += _chunk_mixers(ci, carry, z_ref, o_ref, mask_ref, negmask_ref, tri, shift_ref, wgate_ref, bgate_ref,
                                 gnorm_ref, lb_ref, hnorm_ref, conv_ref, gateb_ref, mnorm_ref, lane, everything,
                                 bsz=bsz, n_pieces=n_pieces, rows=rows)
        shared = carry.out
    while pending:
        for gen in list(pending):
            try:
                next(gen)
            except StopIteration:
                pending.remove(gen)


def _chunk_mixers(ci, carry, z_ref, o_ref, mask_ref, negmask_ref, tri, shift_ref, wgate_ref, bgate_ref, gnorm_ref,
                  lb_ref, hnorm_ref, conv_ref, gateb_ref, mnorm_ref, lane, everything, *, bsz, n_pieces, rows):
    def load(off, width):
        return jnp.concatenate([z_ref[b, ci, i, :, off:off + width] for i in range(n_pieces) for b in range(bsz)],
                               axis=0)

    def store(tile_idx, val):
        val = val.astype(o_ref.dtype)
        for i in range(n_pieces):
            for b in range(bsz):
                r0 = (i * bsz + b) * PIECE
                o_ref[b, ci, i, :, tile_idx * LANE:(tile_idx + 1) * LANE] = val[r0:r0 + PIECE]

    small = load(SMALL, LANE)

    def gla_group():
        logits = _dot(small, wgate_ref[...]) + bgate_ref[...]
        g = _log2_sigmoid(logits) * (1.0 / GLA_TAU)
        q = load(OFF["gq"], GLA_QK_PAD) * (GLA_DK ** -0.5)
        k = load(OFF["gk"], GLA_QK_PAD)
        outs = yield from _decayed_attention(q, k, g, lambda p: load(OFF["gv"] + p * LANE, LANE), carry, "gla",
                                             mask_ref, tri, bsz, heads=GLA_HEADS, dk=GLA_DK, n_groups=4)
        for p in range(GLA_HEADS // 2):
            o = outs[p]
            o = o * lax.rsqrt(_head_sums(o * o, lane) * (1.0 / HEAD_DIM) + EPS) * gnorm_ref[...]
            o = o * _silu(load(OFF["gout"] + p * LANE, LANE))
            store(p, o)
            yield

    def hgrn_group():
        lb = lb_ref[...]
        f = load(OFF["hf"], HGRN_W)
        t = jnp.exp2(jnp.abs(f) * (-LOG2E))
        u = 1.0 + t
        log_lb = jnp.log2(lb)
        log_rest = jnp.log2(1.0 - lb) + (jnp.minimum(f, 0.0) * LOG2E - jnp.log2(u))
        g = jnp.maximum(log_lb, log_rest) + jnp.log2(1.0 + jnp.exp2(-jnp.abs(log_lb - log_rest)))
        k = (1.0 - lb) * (jnp.where(f > 0.0, t, 1.0) / u)
        q = _silu(load(OFF["hq"], HGRN_W))
        yield
        outs = yield from _decayed_attention(q, k, g, lambda p: load(OFF["hi"] + p * LANE, LANE), carry, "hgrn",
                                             mask_ref, tri, bsz, heads=HGRN_HEADS, dk=HGRN_DK, n_groups=4)
        for p in range(HGRN_HEADS // 2):
            o = outs[p]
            o = o * lax.rsqrt(_head_sums(o * o, lane) * (1.0 / HEAD_DIM) + EPS) * hnorm_ref[...]
            o = o * jax.nn.sigmoid(load(OFF["hout"] + p * LANE, LANE))
            store(GLA_W // LANE + p, o)
            yield

    def mlstm_group():
        x_qk = load(OFF["mqk"], 2 * MLSTM_W)
        x_bf = x_qk.astype(BF16)
        x_prev = yield from carry.read("conv", everything)
        delayed = jnp.dot(shift_ref[...], jnp.concatenate([x_bf, x_prev], axis=0),
                          preferred_element_type=F32)
        carry.write("conv", everything, x_bf)
        yield
        acc = conv_ref[CONV_WIDTH - 1:CONV_WIDTH, :] * x_qk
        for tap in range(CONV_WIDTH - 1):
            acc = acc + conv_ref[tap:tap + 1, :] * delayed[tap * rows:(tap + 1) * rows]
        qk = _silu(acc)
        yield
        pre = small + gateb_ref[...]
        gates = jnp.where(lane < MLSTM_HEADS, _log2_sigmoid(pre),
                          jnp.where(lane < 2 * MLSTM_HEADS, pre * LOG2E, 0.0))
        cum = _cumsum_rows(gates, tri)
        row_form = jnp.where(lane < MLSTM_HEADS, cum, gates).T
        lane_batch = (lax.broadcasted_iota(jnp.int32, (1, rows), 1) // PIECE) % bsz
        row_batch = (lax.broadcasted_iota(jnp.int32, (rows, 1), 0) // PIECE) % bsz
        ones_tile = jnp.ones((rows, LANE), BF16)
        yield
        for p in range(MLSTM_HEADS // 2):
            q_pair = qk[:, p * LANE:(p + 1) * LANE]
            k_pair = qk[:, MLSTM_W + p * LANE:MLSTM_W + (p + 1) * LANE] * (HEAD_DIM ** -0.5)
            kt_pair = k_pair.T
            v2 = jnp.concatenate([load(OFF["mv"] + p * LANE, LANE).astype(BF16), ones_tile], axis=1)
            y_heads = []
            for hh in range(2):
                h = 2 * p + hh
                own = (lane >= hh * HEAD_DIM) & (lane < (hh + 1) * HEAD_DIM)
                b_row = row_form[h:h + 1, :]
                c_row = row_form[MLSTM_HEADS + h:MLSTM_HEADS + h + 1, :] - b_row
                b_col = jnp.sum(jnp.where(lane == h, cum, 0.0), axis=-1, keepdims=True)
                m_prev, c_prev = [], []
                for bb in range(bsz):
                    m_prev.append((yield from carry.read("m", (slice(bb, bb + 1), slice(h, h + 1)))))
                    c_prev.append((yield from carry.read("mlstm", (h, bb))))
                m_prev_col = _per_batch_select(m_prev, row_batch, bsz)
                c_masked = c_row + negmask_ref[...]
                mu = jnp.maximum(m_prev_col, jnp.max(c_masked, axis=-1, keepdims=True))
                q_own = jnp.where(own, q_pair, 0.0)
                prob = jnp.exp2(c_masked - mu) * _dot_nt(q_own, k_pair)
                res = _dot(prob, v2)
                qo = _pieces(q_own, bsz)
                inter_b = []
                for bb in range(bsz):
                    lhs = jnp.concatenate([qo[i][bb] for i in range(n_pieces)], axis=0)
                    inter_b.append(_dot(lhs, c_prev[bb]))
                inter = jnp.concatenate([inter_b[bb][i * PIECE:(i + 1) * PIECE]
                                         for i in range(n_pieces) for bb in range(bsz)], axis=0)
                res = res + jnp.exp2(m_prev_col - mu) * inter
                den = jnp.maximum(jnp.abs(res[:, LANE:]), jnp.exp2(-(b_col + mu)))
                y_heads.append(res[:, :LANE] / den)
                yield
                b_end = [b_col[rows - (bsz - bb) * PIECE + PIECE - 1:rows - (bsz - bb) * PIECE + PIECE, :]
                         for bb in range(bsz)]
                w_row = _per_batch_select(b_end, lane_batch, bsz) + c_row
                m_loc = [jnp.max(jnp.where(lane_batch == bb, w_row, NEG), axis=-1, keepdims=True)
                         for bb in range(bsz)]
                m_new = [jnp.maximum(b_end[bb] + m_prev[bb], m_loc[bb]) for bb in range(bsz)]
                ks_t = (kt_pair * jnp.exp2(w_row - _per_batch_select(m_new, lane_batch, bsz))).astype(BF16)
                lhs = jnp.concatenate([jnp.where(lane_batch == bb, ks_t, jnp.zeros_like(ks_t)) for bb in range(bsz)],
                                      axis=0)
                upd = _dot(lhs, v2)
                for bb in range(bsz):
                    s_old = jnp.exp2(b_end[bb] + m_prev[bb] - m_new[bb])
                    carry.write("mlstm", (h, bb), s_old * c_prev[bb] + upd[bb * LANE:(bb + 1) * LANE])
                    carry.write("m", (slice(bb, bb + 1), slice(h, h + 1)), m_new[bb])
                yield
            y = _halves(lane, y_heads[0], y_heads[1])
            mu_h = _head_sums(y, lane) * (1.0 / HEAD_DIM)
            yc = y - mu_h
            var = _head_sums(yc * yc, lane) * (1.0 / HEAD_DIM)
            y = yc * lax.rsqrt(var + EPS) * mnorm_ref[:, p * LANE:(p + 1) * LANE]
            y = y * jax.nn.sigmoid(load(OFF["mout"] + p * LANE, LANE))
            store((GLA_W + HGRN_W) // LANE + p, y)
            yield

    return [gla_group(), hgrn_group(), mlstm_group()]


def _mixers(z, params, layer):
    bsz, s, nz = z.shape
    d_out = GLA_W + HGRN_W + MLSTM_W
    n_chunks = s // CHUNK
    n_pieces = CHUNK // PIECE
    rows = bsz * CHUNK
    mask = _stack_masks(bsz)
    consts = [jnp.asarray(mask), jnp.asarray((1.0 - mask) * NEG), jnp.asarray(mask, dtype=BF16),
              jnp.asarray(_shift_matrix(bsz), dtype=BF16)]
    whole = lambda a: pl.BlockSpec(a.shape, lambda c: (0,) * a.ndim, pipeline_mode=pl.Buffered(1))
    out = pl.pallas_call(
        functools.partial(_mixer_kernel, bsz=bsz),
        grid=(n_chunks // CHUNKS_PER_STEP,),
        in_specs=([pl.BlockSpec((bsz, CHUNKS_PER_STEP, n_pieces, PIECE, nz), lambda c: (0, c, 0, 0, 0))]
                  + [whole(a) for a in consts] + [_layer_block(a, layer) for a in params]),
        out_specs=pl.BlockSpec((bsz, CHUNKS_PER_STEP, n_pieces, PIECE, d_out), lambda c: (0, c, 0, 0, 0)),
        out_shape=jax.ShapeDtypeStruct((bsz, n_chunks, n_pieces, PIECE, d_out), BF16),
        scratch_shapes=[pltpu.VMEM((GLA_HEADS // 2, bsz, LANE, LANE), F32),
                        pltpu.VMEM((HGRN_HEADS // 2, bsz, LANE, LANE), F32),
                        pltpu.VMEM((MLSTM_HEADS, bsz, LANE, 2 * LANE), F32),
                        pltpu.VMEM((8, LANE), F32),
                        pltpu.VMEM((rows, 2 * MLSTM_W), BF16)],
        compiler_params=pltpu.CompilerParams(dimension_semantics=("arbitrary",),
                                             vmem_limit_bytes=48 * MIB),
        name="token_mixers",
    )(z.reshape(bsz, n_chunks, n_pieces, PIECE, nz), *consts, *params)
    return out.reshape(bsz, s, d_out)


def _out_mlp_kernel(mixed_ref, x_ref, gate1_ref, shift_ref, scale_ref, gate2_ref, gain_ref,
                    wout_ref, w1_ref, w2_ref, fnorm_ref, o_ref, *, final):
    x = x_ref[...] + gate1_ref[...] * jnp.dot(mixed_ref[...], wout_ref[...], preferred_element_type=F32)
    y = _rms(x) * gain_ref[...]
    y = y * (1.0 + scale_ref[...]) + shift_ref[...]
    u = jnp.dot(y.astype(BF16), w1_ref[...], preferred_element_type=F32)
    u = jnp.square(jnp.maximum(u, 0.0))
    x = x + gate2_ref[...] * jnp.dot(u.astype(BF16), w2_ref[...], preferred_element_type=F32)
    if final:
        x = _rms(x) * fnorm_ref[...]
    o_ref[...] = x


def _out_mlp(mixed, x, gate1, shift, scale, gate2, gains, w_out, w1, w2, fnorm, layer, *, final, tm=512):
    bsz, s, d = x.shape
    tok = pl.BlockSpec((None, tm, d), lambda b, i: (b, i, 0))
    mod = pl.BlockSpec((None, 1, d), lambda b, i: (b, 0, 0))
    return pl.pallas_call(
        functools.partial(_out_mlp_kernel, final=final),
        grid=(bsz, s // tm),
        in_specs=[tok, tok, mod, mod, mod, mod, _layer_block(gains, layer),
                  _layer_block(w_out, layer), _layer_block(w1, layer), _layer_block(w2, layer),
                  pl.BlockSpec((1, d), lambda b, i: (0, 0), pipeline_mode=pl.Buffered(1))],
        out_specs=tok,
        out_shape=jax.ShapeDtypeStruct((bsz, s, d), F32),
        compiler_params=pltpu.CompilerParams(dimension_semantics=("arbitrary", "arbitrary"),
                                             vmem_limit_bytes=56 * MIB),
        name="out_proj_mlp",
    )(mixed, x, gate1, shift, scale, gate2, gains, w_out, w1, w2, fnorm)


def kernel(x, c, w_ada, b_ada, norm_mix, norm_mlp, w_in, gla_w_gate, gla_b_gate, gla_norm, hgrn_lb_logits,
           hgrn_norm, mlstm_conv, mlstm_gate_b, mlstm_norm, w_out, w_ff1, w_ff2, final_norm):
    depth, d, _ = w_ada.shape
    bsz = x.shape[0]
    mod = _adaln(c.astype(F32), w_ada.astype(F32), b_ada.astype(F32))
    mod = mod.reshape(depth, bsz, 6, 1, d)
    lower = _lower_bounds(hgrn_lb_logits.astype(F32))
    w_in_p = _permute_in_proj(w_in)
    gate_b = jnp.concatenate([mlstm_gate_b[:, MLSTM_HEADS:], mlstm_gate_b[:, :MLSTM_HEADS],
                              jnp.zeros((depth, LANE - 2 * MLSTM_HEADS), F32)], axis=-1)
    wgate = jnp.zeros((depth, LANE, GLA_QK_PAD), F32)
    wgate = wgate.at[:, GLOW_LANE:GLOW_LANE + GLA_RANK, :GLA_QK].set(gla_w_gate).astype(BF16)
    row = lambda a: a.reshape(depth, 1, -1).astype(F32)
    mixer_params = [wgate, row(jnp.pad(gla_b_gate, ((0, 0), (0, GLA_QK_PAD - GLA_QK)))),
                    row(jnp.concatenate([gla_norm, gla_norm], axis=-1)),
                    row(lower), row(jnp.concatenate([hgrn_norm, hgrn_norm], axis=-1)),
                    mlstm_conv.astype(F32), row(gate_b), row(mlstm_norm)]
    w_out_b, w1_b, w2_b = w_out.astype(BF16), w_ff1.astype(BF16), w_ff2.astype(BF16)
    gains_mix, gains_mlp = row(norm_mix), row(norm_mlp)
    fnorm = final_norm.reshape(1, d).astype(F32)
    for l in range(depth):
        shift1, scale1, gate1, shift2, scale2, gate2 = (mod[l, :, j] for j in range(6))
        z = _in_proj(x, shift1, scale1, gains_mix, w_in_p, l)
        mixed = _mixers(z, mixer_params, l)
        x = _out_mlp(mixed, x, gate1, shift2, scale2, gate2, gains_mlp, w_out_b, w1_b, w2_b, fnorm, l,
                     final=(l == depth - 1))
    return x
```

```python
import functools

import numpy as np
import jax
import jax.numpy as jnp
from jax import lax
from jax.experimental import pallas as pl
from jax.experimental.pallas import tpu as pltpu

F32 = jnp.float32
BF16 = jnp.bfloat16

HEAD_DIM = 64
GLA_HEADS = 6
GLA_DK = 32
GLA_RANK = 16
GLA_TAU = 16.0
HGRN_HEADS = 6
HGRN_DK = 64
MLSTM_HEADS = 4
CONV_WIDTH = 4
EPS = 1e-6
CHUNK = 64
CHUNKS_PER_STEP = 2
PIECE = 16
LANE = 128
NEG = -1e30
MAX_EXP2 = 115.0
LOG2E = 1.4426950408889634
MIB = 1024 * 1024

GLA_QK = GLA_HEADS * GLA_DK
GLA_W = GLA_HEADS * HEAD_DIM
HGRN_W = HGRN_HEADS * HEAD_DIM
MLSTM_W = MLSTM_HEADS * HEAD_DIM
GLA_QK_PAD = 2 * LANE

_REF_SEGMENTS = (("gq", GLA_QK), ("gk", GLA_QK), ("gv", GLA_W), ("glow", GLA_RANK), ("gout", GLA_W),
                 ("hq", HGRN_W), ("hf", HGRN_W), ("hi", HGRN_W), ("hout", HGRN_W),
                 ("mqk", 2 * MLSTM_W), ("mv", MLSTM_W), ("mi", MLSTM_HEADS), ("mf", MLSTM_HEADS),
                 ("mout", MLSTM_W))
_MY_ORDER = (("gq", 0), ("pad", GLA_QK_PAD - GLA_QK), ("gk", 0), ("pad", GLA_QK_PAD - GLA_QK),
             ("gv", 0), ("gout", 0), ("hq", 0), ("hf", 0), ("hi", 0), ("hout", 0),
             ("mqk", 0), ("mv", 0), ("mout", 0), ("mf", 0), ("mi", 0), ("glow", 0))


def _layout():
    ref_off, off = {}, 0
    for name, width in _REF_SEGMENTS:
        ref_off[name] = (off, width)
        off += width
    my_off, off = {}, 0
    for name, pad in _MY_ORDER:
        if name == "pad":
            off += pad
        else:
            my_off[name] = off
            off += ref_off[name][1]
    padded = -(-off // LANE) * LANE
    return ref_off, my_off, off, padded


_REF_OFF, OFF, _NZ_USED, NZ = _layout()
SMALL = OFF["mf"]
GLOW_LANE = OFF["glow"] - SMALL
assert SMALL % LANE == 0 and OFF["mi"] == SMALL + MLSTM_HEADS and GLOW_LANE == 2 * MLSTM_HEADS
assert all(OFF[n] % LANE == 0 for n in ("gq", "gk", "gv", "gout", "hq", "hf", "hi", "hout", "mqk", "mv", "mout"))


def _permute_kernel(wt_ref, o_ref, rows_scr):
    off = 0
    for name, pad in _MY_ORDER:
        if name == "pad":
            rows_scr[off:off + pad, :] = jnp.zeros((pad, rows_scr.shape[1]), F32)
            off += pad
        else:
            lo, width = _REF_OFF[name]
            rows_scr[off:off + width, :] = wt_ref[lo:lo + width, :]
            off += width
    rows_scr[off:, :] = jnp.zeros((rows_scr.shape[0] - off, rows_scr.shape[1]), F32)
    for t in range(rows_scr.shape[0] // LANE):
        o_ref[:, t * LANE:(t + 1) * LANE] = rows_scr[t * LANE:(t + 1) * LANE, :].T.astype(o_ref.dtype)


def _permute_in_proj(w_in, *, tc=256):
    depth, d, d_in = w_in.shape
    return pl.pallas_call(
        _permute_kernel,
        grid=(depth, d // tc),
        in_specs=[pl.BlockSpec((None, d_in, tc), lambda l, i: (l, 0, i))],
        out_specs=pl.BlockSpec((None, tc, NZ), lambda l, i: (l, i, 0)),
        out_shape=jax.ShapeDtypeStruct((depth, d, NZ), BF16),
        scratch_shapes=[pltpu.VMEM((NZ, tc), F32)],
        name="permute_in_proj_weight",
    )(jnp.swapaxes(w_in, 1, 2))


def _silu(x):
    return x * jax.nn.sigmoid(x)


def _log2_sigmoid(x):
    return jnp.minimum(x, 0.0) * LOG2E - jnp.log2(1.0 + jnp.exp2(jnp.abs(x) * (-LOG2E)))


def _dot(a, b):
    return jnp.dot(a.astype(BF16), b.astype(BF16), preferred_element_type=F32)


def _dot_nt(a, b):
    return lax.dot_general(a.astype(BF16), b.astype(BF16), (((1,), (1,)), ((), ())),
                           preferred_element_type=F32)


def _cumsum_rows(g, tri):
    hi = g.astype(BF16)
    lo = (g - hi.astype(F32)).astype(BF16)
    return (jnp.dot(tri, hi, preferred_element_type=F32) + jnp.dot(tri, lo, preferred_element_type=F32))


def _adaln_kernel(c_ref, w_ref, b_ref, o_ref):
    cond = _silu(c_ref[...])
    o_ref[...] = _dot(cond, w_ref[...]) + b_ref[...]


def _adaln(c, w_ada, b_ada, *, tn=2048):
    depth, d, n = w_ada.shape
    bsz = c.shape[0]
    return pl.pallas_call(
        _adaln_kernel,
        grid=(depth, n // tn),
        in_specs=[pl.BlockSpec((bsz, d), lambda l, j: (0, 0)),
                  pl.BlockSpec((None, d, tn), lambda l, j: (l, 0, j)),
                  pl.BlockSpec((None, 1, tn), lambda l, j: (l, 0, j))],
        out_specs=pl.BlockSpec((None, bsz, tn), lambda l, j: (l, 0, j)),
        out_shape=jax.ShapeDtypeStruct((depth, bsz, n), F32),
        name="adaln_modulation",
    )(c, w_ada, b_ada.reshape(depth, 1, n))


def _lower_bound_kernel(logit_ref, o_ref):
    x = logit_ref[...]
    e = jnp.exp(x - jnp.max(x, axis=0, keepdims=True))
    p = e / jnp.sum(e, axis=0, keepdims=True)
    depth = x.shape[0]
    acc = p[0:1]
    rows = [acc]
    for l in range(1, depth):
        acc = acc + p[l:l + 1]
        rows.append(acc)
    cum = jnp.concatenate(rows, axis=0)
    o_ref[...] = cum - cum[0:1]


def _lower_bounds(lb_logits):
    return pl.pallas_call(
        _lower_bound_kernel,
        out_shape=jax.ShapeDtypeStruct(lb_logits.shape, F32),
        name="hgrn_lower_bounds",
    )(lb_logits)


def _rms(x):
    return x * lax.rsqrt(jnp.mean(x * x, axis=-1, keepdims=True) + EPS)


def _layer_block(arr, layer):
    tail = arr.shape[1:]
    return pl.BlockSpec((None,) + tail, lambda *_: (layer,) + (0,) * len(tail), pipeline_mode=pl.Buffered(1))


def _in_proj_kernel(x_ref, shift_ref, scale_ref, gain_ref, w_ref, *rest):
    n_cast = (len(rest) - 1) // 2
    z_ref = rest[n_cast]
    y = _rms(x_ref[...]) * gain_ref[...]
    y = y * (1.0 + scale_ref[...]) + shift_ref[...]
    z_ref[...] = jnp.dot(y.astype(BF16), w_ref[...], preferred_element_type=F32)
    for src, dst in zip(rest[:n_cast], rest[n_cast + 1:]):
        dst[...] = src[...].astype(dst.dtype)


def _in_proj(x, shift, scale, gains, w, layer, cast, *, tm=512):
    bsz, s, d = x.shape
    nz = w.shape[-1]
    n_i = s // tm
    steps = bsz * n_i
    slab = lambda a: a.shape[1] // steps
    outs = pl.pallas_call(
        _in_proj_kernel,
        grid=(bsz, n_i),
        in_specs=([pl.BlockSpec((None, tm, d), lambda b, i: (b, i, 0)),
                   pl.BlockSpec((None, 1, d), lambda b, i: (b, 0, 0)),
                   pl.BlockSpec((None, 1, d), lambda b, i: (b, 0, 0)),
                   _layer_block(gains, layer),
                   _layer_block(w, layer)]
                  + [pl.BlockSpec((None, slab(a), a.shape[2]), lambda b, i: (layer, b * n_i + i, 0)) for a in cast]),
        out_specs=([pl.BlockSpec((None, tm, nz), lambda b, i: (b, i, 0))]
                   + [pl.BlockSpec((slab(a), a.shape[2]), lambda b, i: (b * n_i + i, 0)) for a in cast]),
        out_shape=([jax.ShapeDtypeStruct((bsz, s, nz), F32)]
                   + [jax.ShapeDtypeStruct(a.shape[1:], BF16) for a in cast]),
        compiler_params=pltpu.CompilerParams(dimension_semantics=("arbitrary", "arbitrary"),
                                             vmem_limit_bytes=48 * MIB),
        name="norm_in_proj",
    )(x, shift, scale, gains, w, *cast)
    return outs[0], outs[1:]


def _stack_masks(bsz):
    r = np.arange(bsz * CHUNK)
    batch = (r // PIECE) % bsz
    pos = (r // (PIECE * bsz)) * PIECE + r % PIECE
    return ((batch[:, None] == batch[None, :]) & (pos[None, :] <= pos[:, None])).astype(np.float32)


def _shift_matrix(bsz):
    rows = bsz * CHUNK
    r = np.arange(rows)
    batch = (r // PIECE) % bsz
    pos = (r // (PIECE * bsz)) * PIECE + r % PIECE
    same = batch[:, None] == batch[None, :]
    blocks = []
    for d in range(CONV_WIDTH - 1, 0, -1):
        cur = same & (pos[None, :] == pos[:, None] - d)
        prev = same & (pos[None, :] == pos[:, None] - d + CHUNK)
        blocks.append(np.concatenate([cur, prev], axis=1))
    return np.concatenate(blocks, axis=0).astype(np.float32)


def _pieces(x, bsz):
    n_groups = x.shape[0] // (PIECE * bsz)
    return [[x[(i * bsz + b) * PIECE:(i * bsz + b + 1) * PIECE] for b in range(bsz)] for i in range(n_groups)]


def _per_batch_rows(rows, bsz, reps):
    slab = jnp.concatenate([jnp.broadcast_to(rows[b], (PIECE, rows[b].shape[-1])) for b in range(bsz)], axis=0)
    return jnp.concatenate([slab] * reps, axis=0) if reps > 1 else slab


def _per_batch_select(vals, batch_ids, bsz):
    out = vals[0]
    for b in range(1, bsz):
        out = jnp.where(batch_ids == b, vals[b], out)
    return out


def _halves(lane, even, odd):
    return jnp.where(lane < HEAD_DIM, even, odd)


def _head_sums(x, lane):
    lo = jnp.sum(jnp.where(lane < HEAD_DIM, x, 0.0), axis=-1, keepdims=True)
    hi = jnp.sum(jnp.where(lane < HEAD_DIM, 0.0, x), axis=-1, keepdims=True)
    return _halves(lane, lo, hi)


class _Carry:
    def __init__(self, refs, shared, last):
        self.refs, self.shared, self.last = refs, shared, last
        self.out = {}

    def read(self, name, idx):
        if self.shared is None:
            return self.refs[name][idx]
        while (name, idx) not in self.shared:
            yield
        return self.shared[(name, idx)]

    def write(self, name, idx, val):
        if self.last:
            self.refs[name][idx] = val
        else:
            self.out[(name, idx)] = val


def _decayed_attention(q, k, g, v_tile, carry, name, mask_ref, tri, bsz, *, heads, dk, n_groups):
    rows, wk = q.shape
    glen = CHUNK // n_groups
    gp = glen // PIECE
    grows = rows // n_groups
    b = _cumsum_rows(g, tri)
    bp = _pieces(b, bsz)

    def ref_rows(gi):
        mid = gi * glen + glen // 2
        return _per_batch_rows([bp[mid // PIECE][bb][mid % PIECE:mid % PIECE + 1] for bb in range(bsz)], bsz, gp)

    refs = [ref_rows(gi) for gi in range(n_groups)]
    q_in = q * jnp.exp2(jnp.minimum(b - jnp.concatenate(refs, axis=0), MAX_EXP2))
    q_state = q * jnp.exp2(b)
    last = [bp[-1][bb][PIECE - 1:PIECE] for bb in range(bsz)]
    k_state = (k * jnp.exp2(_per_batch_rows(last, bsz, rows // (bsz * PIECE)) - b)).astype(BF16)
    yield
    k_in = []
    for gi in range(n_groups):
        lo, end = gi * grows, (gi + 1) * grows
        own = k[lo:end] * jnp.exp2(jnp.minimum(refs[gi] - b[lo:end], MAX_EXP2))
        if gi:
            ref = jnp.concatenate([refs[gi]] * gi, axis=0) if gi > 1 else refs[gi]
            own = jnp.concatenate([k[:lo] * jnp.exp2(ref - b[:lo]), own], axis=0)
        k_in.append(own.astype(BF16))
    yield

    lane = lax.broadcasted_iota(jnp.int32, (1, LANE), 1)
    heads_per_tile = LANE // dk
    n_pairs = heads // 2
    probs = [None] * heads
    for kt in range(-(-heads // heads_per_tile)):
        tile_heads = list(range(kt * heads_per_tile, min(heads, (kt + 1) * heads_per_tile)))
        tl = slice(kt * LANE, (kt + 1) * LANE)
        per_head = {h: [] for h in tile_heads}
        for gi in range(n_groups):
            end = (gi + 1) * grows
            qg = q_in[gi * grows:end, tl]
            lhs = jnp.concatenate(
                [jnp.where((lane >= (h % heads_per_tile) * dk) & (lane < (h % heads_per_tile + 1) * dk), qg, 0.0)
                 for h in tile_heads], axis=0).astype(BF16)
            sc = _dot_nt(lhs, k_in[gi][:, tl])
            keep = mask_ref[gi * grows:end, :end]
            for n, h in enumerate(tile_heads):
                blk = (sc[n * grows:(n + 1) * grows] * keep).astype(BF16)
                if end < rows:
                    blk = jnp.concatenate([blk, jnp.zeros((grows, rows - end), BF16)], axis=1)
                per_head[h].append(blk)
        for h in tile_heads:
            probs[h] = jnp.concatenate(per_head[h], axis=0)
        yield

    lane_batch = (lax.broadcasted_iota(jnp.int32, (1, rows), 1) // PIECE) % bsz
    row_i = lax.broadcasted_iota(jnp.int32, (LANE, LANE), 0)
    col_i = lax.broadcasted_iota(jnp.int32, (LANE, LANE), 1)
    outs = []
    for p in range(n_pairs):
        kt = (2 * p) // heads_per_tile
        tl = slice(kt * LANE, (kt + 1) * LANE)
        v = v_tile(p)
        vb = v.astype(BF16)
        intra = _halves(lane, _dot(probs[2 * p], vb), _dot(probs[2 * p + 1], vb))
        qs = _pieces(q_state[:, tl], bsz)
        inter_b, states = [], []
        for bb in range(bsz):
            lhs = jnp.concatenate([qs[i][bb] for i in range(len(qs))], axis=0)
            states.append((yield from carry.read(name, (p, bb))))
            inter_b.append(_dot_nt(lhs, states[bb]))
        inter = jnp.concatenate([inter_b[bb][i * PIECE:(i + 1) * PIECE]
                                 for i in range(len(qs)) for bb in range(bsz)], axis=0)
        outs.append(intra + inter)
        yield
        vt = v.T.astype(BF16)
        lhs = jnp.concatenate([jnp.where(lane_batch == bb, vt, jnp.zeros_like(vt)) for bb in range(bsz)], axis=0)
        upd = _dot(lhs, k_state[:, tl])
        same_head = (2 * p + row_i // HEAD_DIM) == (kt * heads_per_tile + col_i // dk)
        for bb in range(bsz):
            decay = jnp.exp2(last[bb][:, tl])
            carry.write(name, (p, bb),
                        states[bb] * decay + jnp.where(same_head, upd[bb * LANE:(bb + 1) * LANE], 0.0))
        yield
    return outs


def _mixer_kernel(z_ref, mask_ref, negmask_ref, tri_ref, shift_ref, wgate_ref, bgate_ref, gnorm_ref, lb_ref, hnorm_ref,
                  conv_ref, gateb_ref, mnorm_ref, o_ref, gla_state, hgrn_state, mlstm_state, mlstm_m, conv_carry,
                  *, bsz):
    @pl.when(pl.program_id(0) == 0)
    def _():
        gla_state[...] = jnp.zeros_like(gla_state)
        hgrn_state[...] = jnp.zeros_like(hgrn_state)
        mlstm_state[...] = jnp.zeros_like(mlstm_state)
        mlstm_m[...] = jnp.zeros_like(mlstm_m)
        conv_carry[...] = jnp.zeros_like(conv_carry)

    n_pieces = CHUNK // PIECE
    rows = bsz * CHUNK
    tri = tri_ref[...]
    lane = lax.broadcasted_iota(jnp.int32, (1, LANE), 1)
    refs = {"gla": gla_state, "hgrn": hgrn_state, "mlstm": mlstm_state, "m": mlstm_m, "conv": conv_carry}
    everything = (Ellipsis,)

    pending, shared = [], None
    for ci in range(CHUNKS_PER_STEP):
        carry = _Carry(refs, shared, last=(ci == CHUNKS_PER_STEP - 1))
        pending += _chunk_mixers(ci, carry, z_ref, o_ref, mask_ref, negmask_ref, tri, shift_ref, wgate_ref, bgate_ref,
                                 gnorm_ref, lb_ref, hnorm_ref, conv_ref, gateb_ref, mnorm_ref, lane, everything,
                                 bsz=bsz, n_pieces=n_pieces, rows=rows)
        shared = carry.out
    while pending:
        for gen in list(pending):
            try:
                next(gen)
            except StopIteration:
                pending.remove(gen)


def _chunk_mixers(ci, carry, z_ref, o_ref, mask_ref, negmask_ref, tri, shift_ref, wgate_ref, bgate_ref, gnorm_ref,
                  lb_ref, hnorm_ref, conv_ref, gateb_ref, mnorm_ref, lane, everything, *, bsz, n_pieces, rows):
    def load(off, width):
        return jnp.concatenate([z_ref[b, ci, i, :, off:off + width] for i in range(n_pieces) for b in range(bsz)],
                               axis=0)

    def store(tile_idx, val):
        val = val.astype(o_ref.dtype)
        for i in range(n_pieces):
            for b in range(bsz):
                r0 = (i * bsz + b) * PIECE
                o_ref[b, ci, i, :, tile_idx * LANE:(tile_idx + 1) * LANE] = val[r0:r0 + PIECE]

    small = load(SMALL, LANE)

    def gla_group():
        logits = _dot(small, wgate_ref[...]) + bgate_ref[...]
        g = _log2_sigmoid(logits) * (1.0 / GLA_TAU)
        q = load(OFF["gq"], GLA_QK_PAD) * (GLA_DK ** -0.5)
        k = load(OFF["gk"], GLA_QK_PAD)
        outs = yield from _decayed_attention(q, k, g, lambda p: load(OFF["gv"] + p * LANE, LANE), carry, "gla",
                                             mask_ref, tri, bsz, heads=GLA_HEADS, dk=GLA_DK, n_groups=4)
        for p in range(GLA_HEADS // 2):
            o = outs[p]
            o = o * lax.rsqrt(_head_sums(o * o, lane) + HEAD_DIM * EPS) * gnorm_ref[...]
            o = o * _silu(load(OFF["gout"] + p * LANE, LANE))
            store(p, o)
            yield

    def hgrn_group():
        lb = lb_ref[...]
        f = load(OFF["hf"], HGRN_W)
        t = jnp.exp2(jnp.abs(f) * (-LOG2E))
        u = 1.0 + t
        log_lb = jnp.log2(lb)
        log_rest = jnp.log2(1.0 - lb) + (jnp.minimum(f, 0.0) * LOG2E - jnp.log2(u))
        g = jnp.maximum(log_lb, log_rest) + jnp.log2(1.0 + jnp.exp2(-jnp.abs(log_lb - log_rest)))
        k = (1.0 - lb) * (jnp.where(f > 0.0, t, 1.0) / u)
        q = _silu(load(OFF["hq"], HGRN_W))
        yield
        outs = yield from _decayed_attention(q, k, g, lambda p: load(OFF["hi"] + p * LANE, LANE), carry, "hgrn",
                                             mask_ref, tri, bsz, heads=HGRN_HEADS, dk=HGRN_DK, n_groups=4)
        for p in range(HGRN_HEADS // 2):
            o = outs[p]
            o = o * lax.rsqrt(_head_sums(o * o, lane) + HEAD_DIM * EPS) * hnorm_ref[...]
            o = o * jax.nn.sigmoid(load(OFF["hout"] + p * LANE, LANE))
            store(GLA_W // LANE + p, o)
            yield

    def mlstm_group():
        x_qk = load(OFF["mqk"], 2 * MLSTM_W)
        x_bf = x_qk.astype(BF16)
        x_prev = yield from carry.read("conv", everything)
        delayed = jnp.dot(shift_ref[...], jnp.concatenate([x_bf, x_prev], axis=0),
                          preferred_element_type=F32)
        carry.write("conv", everything, x_bf)
        yield
        acc = conv_ref[CONV_WIDTH - 1:CONV_WIDTH, :] * x_qk
        for tap in range(CONV_WIDTH - 1):
            acc = acc + conv_ref[tap:tap + 1, :] * delayed[tap * rows:(tap + 1) * rows]
        qk = _silu(acc)
        yield
        pre = small + gateb_ref[...]
        gates = jnp.where(lane < MLSTM_HEADS, _log2_sigmoid(pre),
                          jnp.where(lane < 2 * MLSTM_HEADS, pre * LOG2E, 0.0))
        cum = _cumsum_rows(gates, tri)
        row_form = jnp.where(lane < MLSTM_HEADS, cum, gates).T
        lane_batch = (lax.broadcasted_iota(jnp.int32, (1, rows), 1) // PIECE) % bsz
        row_batch = (lax.broadcasted_iota(jnp.int32, (rows, 1), 0) // PIECE) % bsz
        ones_tile = jnp.ones((rows, LANE), BF16)
        yield
        for p in range(MLSTM_HEADS // 2):
            q_pair = qk[:, p * LANE:(p + 1) * LANE]
            k_pair = qk[:, MLSTM_W + p * LANE:MLSTM_W + (p + 1) * LANE] * (HEAD_DIM ** -0.5)
            kt_pair = k_pair.T
            v2 = jnp.concatenate([load(OFF["mv"] + p * LANE, LANE).astype(BF16), ones_tile], axis=1)
            y_heads = []
            for hh in range(2):
                h = 2 * p + hh
                own = (lane >= hh * HEAD_DIM) & (lane < (hh + 1) * HEAD_DIM)
                b_row = row_form[h:h + 1, :]
                c_row = row_form[MLSTM_HEADS + h:MLSTM_HEADS + h + 1, :] - b_row
                b_col = jnp.sum(jnp.where(lane == h, cum, 0.0), axis=-1, keepdims=True)
                m_prev, c_prev = [], []
                for bb in range(bsz):
                    m_prev.append((yield from carry.read("m", (slice(bb, bb + 1), slice(h, h + 1)))))
                    c_prev.append((yield from carry.read("mlstm", (h, bb))))
                m_prev_col = _per_batch_select(m_prev, row_batch, bsz)
                c_masked = c_row + negmask_ref[...]
                mu = jnp.maximum(m_prev_col, jnp.max(c_masked, axis=-1, keepdims=True))
                q_own = jnp.where(own, q_pair, 0.0)
                prob = jnp.exp2(c_masked - mu) * _dot_nt(q_own, k_pair)
                res = _dot(prob, v2)
                qo = _pieces(q_own, bsz)
                inter_b = []
                for bb in range(bsz):
                    lhs = jnp.concatenate([qo[i][bb] for i in range(n_pieces)], axis=0)
                    inter_b.append(_dot(lhs, c_prev[bb]))
                inter = jnp.concatenate([inter_b[bb][i * PIECE:(i + 1) * PIECE]
                                         for i in range(n_pieces) for bb in range(bsz)], axis=0)
                res = res + jnp.exp2(m_prev_col - mu) * inter
                den = jnp.maximum(jnp.abs(res[:, LANE:]), jnp.exp2(-(b_col + mu)))
                y_heads.append(res[:, :LANE] / den)
                yield
                b_end = [b_col[rows - (bsz - bb) * PIECE + PIECE - 1:rows - (bsz - bb) * PIECE + PIECE, :]
                         for bb in range(bsz)]
                w_row = _per_batch_select(b_end, lane_batch, bsz) + c_row
                m_loc = [jnp.max(jnp.where(lane_batch == bb, w_row, NEG), axis=-1, keepdims=True)
                         for bb in range(bsz)]
                m_new = [jnp.maximum(b_end[bb] + m_prev[bb], m_loc[bb]) for bb in range(bsz)]
                ks_t = (kt_pair * jnp.exp2(w_row - _per_batch_select(m_new, lane_batch, bsz))).astype(BF16)
                lhs = jnp.concatenate([jnp.where(lane_batch == bb, ks_t, jnp.zeros_like(ks_t)) for bb in range(bsz)],
                                      axis=0)
                upd = _dot(lhs, v2)
                for bb in range(bsz):
                    s_old = jnp.exp2(b_end[bb] + m_prev[bb] - m_new[bb])
                    carry.write("mlstm", (h, bb), s_old * c_prev[bb] + upd[bb * LANE:(bb + 1) * LANE])
                    carry.write("m", (slice(bb, bb + 1), slice(h, h + 1)), m_new[bb])
                yield
            y = _halves(lane, y_heads[0], y_heads[1])
            mu_h = _head_sums(y, lane) * (1.0 / HEAD_DIM)
            yc = y - mu_h
            sq = _head_sums(yc * yc, lane)
            y = yc * lax.rsqrt(sq + HEAD_DIM * EPS) * mnorm_ref[:, p * LANE:(p + 1) * LANE]
            y = y * jax.nn.sigmoid(load(OFF["mout"] + p * LANE, LANE))
            store((GLA_W + HGRN_W) // LANE + p, y)
            yield

    return [gla_group(), hgrn_group(), mlstm_group()]


def _mixers(z, params, layer):
    bsz, s, nz = z.shape
    d_out = GLA_W + HGRN_W + MLSTM_W
    n_chunks = s // CHUNK
    n_pieces = CHUNK // PIECE
    rows = bsz * CHUNK
    mask = _stack_masks(bsz)
    consts = [jnp.asarray(mask), jnp.asarray((1.0 - mask) * NEG), jnp.asarray(mask, dtype=BF16),
              jnp.asarray(_shift_matrix(bsz), dtype=BF16)]
    whole = lambda a: pl.BlockSpec(a.shape, lambda c: (0,) * a.ndim, pipeline_mode=pl.Buffered(1))
    out = pl.pallas_call(
        functools.partial(_mixer_kernel, bsz=bsz),
        grid=(n_chunks // CHUNKS_PER_STEP,),
        in_specs=([pl.BlockSpec((bsz, CHUNKS_PER_STEP, n_pieces, PIECE, nz), lambda c: (0, c, 0, 0, 0))]
                  + [whole(a) for a in consts] + [_layer_block(a, layer) for a in params]),
        out_specs=pl.BlockSpec((bsz, CHUNKS_PER_STEP, n_pieces, PIECE, d_out), lambda c: (0, c, 0, 0, 0)),
        out_shape=jax.ShapeDtypeStruct((bsz, n_chunks, n_pieces, PIECE, d_out), BF16),
        scratch_shapes=[pltpu.VMEM((GLA_HEADS // 2, bsz, LANE, LANE), F32),
                        pltpu.VMEM((HGRN_HEADS // 2, bsz, LANE, LANE), F32),
                        pltpu.VMEM((MLSTM_HEADS, bsz, LANE, 2 * LANE), F32),
                        pltpu.VMEM((8, LANE), F32),
                        pltpu.VMEM((rows, 2 * MLSTM_W), BF16)],
        compiler_params=pltpu.CompilerParams(dimension_semantics=("arbitrary",),
                                             vmem_limit_bytes=48 * MIB),
        name="token_mixers",
    )(z.reshape(bsz, n_chunks, n_pieces, PIECE, nz), *consts, *params)
    return out.reshape(bsz, s, d_out)


def _out_mlp_kernel(mixed_ref, x_ref, gate1_ref, shift_ref, scale_ref, gate2_ref, gain_ref,
                    wout_ref, w1_ref, w2_ref, fnorm_ref, o_ref, *, final):
    x = x_ref[...] + gate1_ref[...] * jnp.dot(mixed_ref[...], wout_ref[...], preferred_element_type=F32)
    y = _rms(x) * gain_ref[...]
    y = y * (1.0 + scale_ref[...]) + shift_ref[...]
    u = jnp.dot(y.astype(BF16), w1_ref[...], preferred_element_type=F32)
    u = jnp.square(jnp.maximum(u, 0.0))
    x = x + gate2_ref[...] * jnp.dot(u.astype(BF16), w2_ref[...], preferred_element_type=F32)
    if final:
        x = _rms(x) * fnorm_ref[...]
    o_ref[...] = x


def _out_mlp(mixed, x, gate1, shift, scale, gate2, gains, w_out, w1, w2, fnorm, layer, *, final, tm=512):
    bsz, s, d = x.shape
    tok = pl.BlockSpec((None, tm, d), lambda b, i: (b, i, 0))
    mod = pl.BlockSpec((None, 1, d), lambda b, i: (b, 0, 0))
    whole = lambda a: pl.BlockSpec(a.shape, lambda b, i: (0,) * a.ndim, pipeline_mode=pl.Buffered(1))
    return pl.pallas_call(
        functools.partial(_out_mlp_kernel, final=final),
        grid=(bsz, s // tm),
        in_specs=[tok, tok, mod, mod, mod, mod, _layer_block(gains, layer),
                  whole(w_out), whole(w1), whole(w2), whole(fnorm)],
        out_specs=tok,
        out_shape=jax.ShapeDtypeStruct((bsz, s, d), F32),
        compiler_params=pltpu.CompilerParams(dimension_semantics=("arbitrary", "arbitrary"),
                                             vmem_limit_bytes=56 * MIB),
        name="out_proj_mlp",
    )(mixed, x, gate1, shift, scale, gate2, gains, w_out, w1, w2, fnorm)


def kernel(x, c, w_ada, b_ada, norm_mix, norm_mlp, w_in, gla_w_gate, gla_b_gate, gla_norm, hgrn_lb_logits,
           hgrn_norm, mlstm_conv, mlstm_gate_b, mlstm_norm, w_out, w_ff1, w_ff2, final_norm):
    depth, d, _ = w_ada.shape
    bsz = x.shape[0]
    mod = _adaln(c.astype(F32), w_ada.astype(F32), b_ada.astype(F32))
    mod = mod.reshape(depth, bsz, 6, 1, d)
    lower = _lower_bounds(hgrn_lb_logits.astype(F32))
    w_in_p = _permute_in_proj(w_in)
    gate_b = jnp.concatenate([mlstm_gate_b[:, MLSTM_HEADS:], mlstm_gate_b[:, :MLSTM_HEADS],
                              jnp.zeros((depth, LANE - 2 * MLSTM_HEADS), F32)], axis=-1)
    wgate = jnp.zeros((depth, LANE, GLA_QK_PAD), F32)
    wgate = wgate.at[:, GLOW_LANE:GLOW_LANE + GLA_RANK, :GLA_QK].set(gla_w_gate).astype(BF16)
    row = lambda a: a.reshape(depth, 1, -1).astype(F32)
    mixer_params = [wgate, row(jnp.pad(gla_b_gate, ((0, 0), (0, GLA_QK_PAD - GLA_QK)))),
                    row(jnp.concatenate([gla_norm, gla_norm], axis=-1) * HEAD_DIM ** 0.5),
                    row(lower), row(jnp.concatenate([hgrn_norm, hgrn_norm], axis=-1) * HEAD_DIM ** 0.5),
                    mlstm_conv.astype(F32), row(gate_b), row(mlstm_norm * HEAD_DIM ** 0.5)]
    late_weights = [w_out.astype(F32), w_ff1.astype(F32), w_ff2.astype(F32)]
    gains_mix, gains_mlp = row(norm_mix), row(norm_mlp)
    fnorm = final_norm.reshape(1, d).astype(F32)
    for l in range(depth):
        shift1, scale1, gate1, shift2, scale2, gate2 = (mod[l, :, j] for j in range(6))
        z, (w_out_b, w1_b, w2_b) = _in_proj(x, shift1, scale1, gains_mix, w_in_p, l, late_weights)
        mixed = _mixers(z, mixer_params, l)
        x = _out_mlp(mixed, x, gate1, shift2, scale2, gate2, gains_mlp, w_out_b, w1_b, w2_b, fnorm, l,
                     final=(l == depth - 1))
    return x
```

```python
import functools

import numpy as np
import jax
import jax.numpy as jnp
from jax import lax
from jax.experimental import pallas as pl
from jax.experimental.pallas import tpu as pltpu

F32 = jnp.float32
BF16 = jnp.bfloat16

HEAD_DIM = 64
GLA_HEADS = 6
GLA_DK = 32
GLA_RANK = 16
GLA_TAU = 16.0
HGRN_HEADS = 6
HGRN_DK = 64
MLSTM_HEADS = 4
CONV_WIDTH = 4
EPS = 1e-6
CHUNK = 64
CHUNKS_PER_STEP = 2
PIECE = 16
LANE = 128
NEG = -1e30
MAX_EXP2 = 115.0
LOG2E = 1.4426950408889634
MIB = 1024 * 1024

GLA_QK = GLA_HEADS * GLA_DK
GLA_W = GLA_HEADS * HEAD_DIM
HGRN_W = HGRN_HEADS * HEAD_DIM
MLSTM_W = MLSTM_HEADS * HEAD_DIM
GLA_QK_PAD = 2 * LANE

_REF_SEGMENTS = (("gq", GLA_QK), ("gk", GLA_QK), ("gv", GLA_W), ("glow", GLA_RANK), ("gout", GLA_W),
                 ("hq", HGRN_W), ("hf", HGRN_W), ("hi", HGRN_W), ("hout", HGRN_W),
                 ("mqk", 2 * MLSTM_W), ("mv", MLSTM_W), ("mi", MLSTM_HEADS), ("mf", MLSTM_HEADS),
                 ("mout", MLSTM_W))
_MY_ORDER = (("gq", 0), ("pad", GLA_QK_PAD - GLA_QK), ("gk", 0), ("pad", GLA_QK_PAD - GLA_QK),
             ("gv", 0), ("gout", 0), ("hq", 0), ("hf", 0), ("hi", 0), ("hout", 0),
             ("mqk", 0), ("mv", 0), ("mout", 0), ("mf", 0), ("mi", 0), ("glow", 0))


def _layout():
    ref_off, off = {}, 0
    for name, width in _REF_SEGMENTS:
        ref_off[name] = (off, width)
        off += width
    my_off, off = {}, 0
    for name, pad in _MY_ORDER:
        if name == "pad":
            off += pad
        else:
            my_off[name] = off
            off += ref_off[name][1]
    padded = -(-off // LANE) * LANE
    return ref_off, my_off, off, padded


_REF_OFF, OFF, _NZ_USED, NZ = _layout()
SMALL = OFF["mf"]
GLOW_LANE = OFF["glow"] - SMALL
assert SMALL % LANE == 0 and OFF["mi"] == SMALL + MLSTM_HEADS and GLOW_LANE == 2 * MLSTM_HEADS
assert all(OFF[n] % LANE == 0 for n in ("gq", "gk", "gv", "gout", "hq", "hf", "hi", "hout", "mqk", "mv", "mout"))


def _permute_kernel(wt_ref, o_ref, rows_scr):
    off = 0
    for name, pad in _MY_ORDER:
        if name == "pad":
            rows_scr[off:off + pad, :] = jnp.zeros((pad, rows_scr.shape[1]), F32)
            off += pad
        else:
            lo, width = _REF_OFF[name]
            rows_scr[off:off + width, :] = wt_ref[lo:lo + width, :]
            off += width
    rows_scr[off:, :] = jnp.zeros((rows_scr.shape[0] - off, rows_scr.shape[1]), F32)
    for t in range(rows_scr.shape[0] // LANE):
        o_ref[:, t * LANE:(t + 1) * LANE] = rows_scr[t * LANE:(t + 1) * LANE, :].T.astype(o_ref.dtype)


def _permute_in_proj(w_in, *, tc=256):
    depth, d, d_in = w_in.shape
    return pl.pallas_call(
        _permute_kernel,
        grid=(depth, d // tc),
        in_specs=[pl.BlockSpec((None, d_in, tc), lambda l, i: (l, 0, i))],
        out_specs=pl.BlockSpec((None, tc, NZ), lambda l, i: (l, i, 0)),
        out_shape=jax.ShapeDtypeStruct((depth, d, NZ), BF16),
        scratch_shapes=[pltpu.VMEM((NZ, tc), F32)],
        name="permute_in_proj_weight",
    )(jnp.swapaxes(w_in, 1, 2))


def _silu(x):
    return x * jax.nn.sigmoid(x)


def _log2_sigmoid(x):
    return jnp.minimum(x, 0.0) * LOG2E - jnp.log2(1.0 + jnp.exp2(jnp.abs(x) * (-LOG2E)))


def _dot(a, b):
    return jnp.dot(a.astype(BF16), b.astype(BF16), preferred_element_type=F32)


def _dot_nt(a, b):
    return lax.dot_general(a.astype(BF16), b.astype(BF16), (((1,), (1,)), ((), ())),
                           preferred_element_type=F32)


def _cumsum_rows(g, tri):
    hi = g.astype(BF16)
    lo = (g - hi.astype(F32)).astype(BF16)
    return (jnp.dot(tri, hi, preferred_element_type=F32) + jnp.dot(tri, lo, preferred_element_type=F32))


def _adaln_kernel(c_ref, w_ref, b_ref, o_ref):
    cond = _silu(c_ref[...])
    o_ref[...] = _dot(cond, w_ref[...]) + b_ref[...]


def _adaln(c, w_ada, b_ada, *, tn=2048):
    depth, d, n = w_ada.shape
    bsz = c.shape[0]
    return pl.pallas_call(
        _adaln_kernel,
        grid=(depth, n // tn),
        in_specs=[pl.BlockSpec((bsz, d), lambda l, j: (0, 0)),
                  pl.BlockSpec((None, d, tn), lambda l, j: (l, 0, j)),
                  pl.BlockSpec((None, 1, tn), lambda l, j: (l, 0, j))],
        out_specs=pl.BlockSpec((None, bsz, tn), lambda l, j: (l, 0, j)),
        out_shape=jax.ShapeDtypeStruct((depth, bsz, n), F32),
        name="adaln_modulation",
    )(c, w_ada, b_ada.reshape(depth, 1, n))


def _lower_bound_kernel(logit_ref, o_ref):
    x = logit_ref[...]
    e = jnp.exp(x - jnp.max(x, axis=0, keepdims=True))
    p = e / jnp.sum(e, axis=0, keepdims=True)
    depth = x.shape[0]
    acc = p[0:1]
    rows = [acc]
    for l in range(1, depth):
        acc = acc + p[l:l + 1]
        rows.append(acc)
    cum = jnp.concatenate(rows, axis=0)
    o_ref[...] = cum - cum[0:1]


def _lower_bounds(lb_logits):
    return pl.pallas_call(
        _lower_bound_kernel,
        out_shape=jax.ShapeDtypeStruct(lb_logits.shape, F32),
        name="hgrn_lower_bounds",
    )(lb_logits)


def _rms(x):
    return x * lax.rsqrt(jnp.mean(x * x, axis=-1, keepdims=True) + EPS)


def _layer_block(arr, layer):
    tail = arr.shape[1:]
    return pl.BlockSpec((None,) + tail, lambda *_: (layer,) + (0,) * len(tail), pipeline_mode=pl.Buffered(1))


def _in_proj_kernel(x_ref, shift_ref, scale_ref, gain_ref, w_ref, *rest):
    n_cast = (len(rest) - 1) // 2
    z_ref = rest[n_cast]
    part = x_ref.shape[0] // ROW_PARTS
    ys = []
    for h in range(ROW_PARTS):
        y = _rms(x_ref[h * part:(h + 1) * part, :]) * gain_ref[...]
        ys.append((y * (1.0 + scale_ref[...]) + shift_ref[...]).astype(BF16))
    for h in range(ROW_PARTS):
        z_ref[h * part:(h + 1) * part, :] = jnp.dot(ys[h], w_ref[...], preferred_element_type=F32)
    for src, dst in zip(rest[:n_cast], rest[n_cast + 1:]):
        dst[...] = src[...].astype(dst.dtype)


ROW_PARTS = 2


def _in_proj(x, shift, scale, gains, w, layer, cast, *, tm=1024):
    bsz, s, d = x.shape
    nz = w.shape[-1]
    n_i = s // tm
    steps = bsz * n_i
    slab = lambda a: a.shape[1] // steps
    outs = pl.pallas_call(
        _in_proj_kernel,
        grid=(bsz, n_i),
        in_specs=([pl.BlockSpec((None, tm, d), lambda b, i: (b, i, 0)),
                   pl.BlockSpec((None, 1, d), lambda b, i: (b, 0, 0)),
                   pl.BlockSpec((None, 1, d), lambda b, i: (b, 0, 0)),
                   _layer_block(gains, layer),
                   _layer_block(w, layer)]
                  + [pl.BlockSpec((None, slab(a), a.shape[2]), lambda b, i: (layer, b * n_i + i, 0)) for a in cast]),
        out_specs=([pl.BlockSpec((None, tm, nz), lambda b, i: (b, i, 0))]
                   + [pl.BlockSpec((slab(a), a.shape[2]), lambda b, i: (b * n_i + i, 0)) for a in cast]),
        out_shape=([jax.ShapeDtypeStruct((bsz, s, nz), F32)]
                   + [jax.ShapeDtypeStruct(a.shape[1:], BF16) for a in cast]),
        compiler_params=pltpu.CompilerParams(dimension_semantics=("arbitrary", "arbitrary"),
                                             vmem_limit_bytes=58 * MIB),
        name="norm_in_proj",
    )(x, shift, scale, gains, w, *cast)
    return outs[0], outs[1:]


def _stack_masks(bsz):
    r = np.arange(bsz * CHUNK)
    batch = (r // PIECE) % bsz
    pos = (r // (PIECE * bsz)) * PIECE + r % PIECE
    return ((batch[:, None] == batch[None, :]) & (pos[None, :] <= pos[:, None])).astype(np.float32)


def _shift_matrix(bsz):
    rows = bsz * CHUNK
    r = np.arange(rows)
    batch = (r // PIECE) % bsz
    pos = (r // (PIECE * bsz)) * PIECE + r % PIECE
    same = batch[:, None] == batch[None, :]
    blocks = []
    for d in range(CONV_WIDTH - 1, 0, -1):
        cur = same & (pos[None, :] == pos[:, None] - d)
        prev = same & (pos[None, :] == pos[:, None] - d + CHUNK)
        blocks.append(np.concatenate([cur, prev], axis=1))
    return np.concatenate(blocks, axis=0).astype(np.float32)


def _pieces(x, bsz):
    n_groups = x.shape[0] // (PIECE * bsz)
    return [[x[(i * bsz + b) * PIECE:(i * bsz + b + 1) * PIECE] for b in range(bsz)] for i in range(n_groups)]


def _per_batch_rows(rows, bsz, reps):
    slab = jnp.concatenate([jnp.broadcast_to(rows[b], (PIECE, rows[b].shape[-1])) for b in range(bsz)], axis=0)
    return jnp.concatenate([slab] * reps, axis=0) if reps > 1 else slab


def _per_batch_select(vals, batch_ids, bsz):
    out = vals[0]
    for b in range(1, bsz):
        out = jnp.where(batch_ids == b, vals[b], out)
    return out


def _halves(lane, even, odd):
    return jnp.where(lane < HEAD_DIM, even, odd)


def _head_sums(x, lane):
    lo = jnp.sum(jnp.where(lane < HEAD_DIM, x, 0.0), axis=-1, keepdims=True)
    hi = jnp.sum(jnp.where(lane < HEAD_DIM, 0.0, x), axis=-1, keepdims=True)
    return _halves(lane, lo, hi)


class _Carry:
    def __init__(self, refs, shared, last):
        self.refs, self.shared, self.last = refs, shared, last
        self.out = {}

    def read(self, name, idx):
        if self.shared is None:
            return self.refs[name][idx]
        while (name, idx) not in self.shared:
            yield
        return self.shared[(name, idx)]

    def write(self, name, idx, val):
        if self.last:
            self.refs[name][idx] = val
        else:
            self.out[(name, idx)] = val


def _decayed_attention(q, k, g, v_tile, carry, name, tri, bsz, *, heads, dk, n_groups):
    rows, wk = q.shape
    glen = CHUNK // n_groups
    gp = glen // PIECE
    grows = rows // n_groups
    b = _cumsum_rows(g, tri)
    bp = _pieces(b, bsz)

    def ref_rows(gi):
        mid = gi * glen + glen // 2
        return _per_batch_rows([bp[mid // PIECE][bb][mid % PIECE:mid % PIECE + 1] for bb in range(bsz)], bsz, gp)

    refs = [ref_rows(gi) for gi in range(n_groups)]
    q_in = q * jnp.exp2(jnp.minimum(b - jnp.concatenate(refs, axis=0), MAX_EXP2))
    q_state = q * jnp.exp2(b)
    last = [bp[-1][bb][PIECE - 1:PIECE] for bb in range(bsz)]
    k_state = (k * jnp.exp2(_per_batch_rows(last, bsz, rows // (bsz * PIECE)) - b)).astype(BF16)
    yield
    k_in = []
    for gi in range(n_groups):
        lo, end = gi * grows, (gi + 1) * grows
        own = k[lo:end] * jnp.exp2(jnp.minimum(refs[gi] - b[lo:end], MAX_EXP2))
        if gi:
            ref = jnp.concatenate([refs[gi]] * gi, axis=0) if gi > 1 else refs[gi]
            own = jnp.concatenate([k[:lo] * jnp.exp2(ref - b[:lo]), own], axis=0)
        k_in.append(own.astype(BF16))
    yield

    lane = lax.broadcasted_iota(jnp.int32, (1, LANE), 1)
    heads_per_tile = LANE // dk
    n_pairs = heads // 2
    probs = [None] * heads
    for kt in range(-(-heads // heads_per_tile)):
        tile_heads = list(range(kt * heads_per_tile, min(heads, (kt + 1) * heads_per_tile)))
        tl = slice(kt * LANE, (kt + 1) * LANE)
        per_head = {h: [] for h in tile_heads}
        for gi in range(n_groups):
            end = (gi + 1) * grows
            qg = q_in[gi * grows:end, tl]
            lhs = jnp.concatenate(
                [jnp.where((lane >= (h % heads_per_tile) * dk) & (lane < (h % heads_per_tile + 1) * dk), qg, 0.0)
                 for h in tile_heads], axis=0).astype(BF16)
            sc = _dot_nt(lhs, k_in[gi][:, tl])
            keep = tri[gi * grows:end, :end] > 0
            for n, h in enumerate(tile_heads):
                blk = sc[n * grows:(n + 1) * grows].astype(BF16)
                blk = jnp.where(keep, blk, jnp.zeros_like(blk))
                if end < rows:
                    blk = jnp.concatenate([blk, jnp.zeros((grows, rows - end), BF16)], axis=1)
                per_head[h].append(blk)
        for h in tile_heads:
            probs[h] = jnp.concatenate(per_head[h], axis=0)
        yield

    lane_batch = (lax.broadcasted_iota(jnp.int32, (1, rows), 1) // PIECE) % bsz
    row_i = lax.broadcasted_iota(jnp.int32, (LANE, LANE), 0)
    col_i = lax.broadcasted_iota(jnp.int32, (LANE, LANE), 1)
    outs = []
    for p in range(n_pairs):
        kt = (2 * p) // heads_per_tile
        tl = slice(kt * LANE, (kt + 1) * LANE)
        v = v_tile(p)
        vb = v.astype(BF16)
        intra = _halves(lane, _dot(probs[2 * p], vb), _dot(probs[2 * p + 1], vb))
        qs = _pieces(q_state[:, tl], bsz)
        inter_b, states = [], []
        for bb in range(bsz):
            lhs = jnp.concatenate([qs[i][bb] for i in range(len(qs))], axis=0)
            states.append((yield from carry.read(name, (p, bb))))
            inter_b.append(_dot_nt(lhs, states[bb]))
        inter = jnp.concatenate([inter_b[bb][i * PIECE:(i + 1) * PIECE]
                                 for i in range(len(qs)) for bb in range(bsz)], axis=0)
        outs.append(intra + inter)
        yield
        vt = v.T.astype(BF16)
        lhs = jnp.concatenate([jnp.where(lane_batch == bb, vt, jnp.zeros_like(vt)) for bb in range(bsz)], axis=0)
        upd = _dot(lhs, k_state[:, tl])
        same_head = (2 * p + row_i // HEAD_DIM) == (kt * heads_per_tile + col_i // dk)
        for bb in range(bsz):
            decay = jnp.exp2(last[bb][:, tl])
            carry.write(name, (p, bb),
                        states[bb] * decay + jnp.where(same_head, upd[bb * LANE:(bb + 1) * LANE], 0.0))
        yield
    return outs


def _mixer_kernel(z_ref, negmask_ref, tri_ref, shift_ref, wgate_ref, bgate_ref, gnorm_ref, lb_ref, hnorm_ref,
                  conv_ref, gateb_ref, mnorm_ref, o_ref, gla_state, hgrn_state, mlstm_state, mlstm_m, conv_carry,
                  *, bsz):
    @pl.when(pl.program_id(0) == 0)
    def _():
        gla_state[...] = jnp.zeros_like(gla_state)
        hgrn_state[...] = jnp.zeros_like(hgrn_state)
        mlstm_state[...] = jnp.zeros_like(mlstm_state)
        mlstm_m[...] = jnp.zeros_like(mlstm_m)
        conv_carry[...] = jnp.zeros_like(conv_carry)

    n_pieces = CHUNK // PIECE
    rows = bsz * CHUNK
    tri = tri_ref[...]
    lane = lax.broadcasted_iota(jnp.int32, (1, LANE), 1)
    refs = {"gla": gla_state, "hgrn": hgrn_state, "mlstm": mlstm_state, "m": mlstm_m, "conv": conv_carry}
    everything = (Ellipsis,)

    pending, shared = [], None
    for ci in range(CHUNKS_PER_STEP):
        carry = _Carry(refs, shared, last=(ci == CHUNKS_PER_STEP - 1))
        pending += _chunk_mixers(ci, carry, z_ref, o_ref, negmask_ref, tri, shift_ref, wgate_ref, bgate_ref,
                                 gnorm_ref, lb_ref, hnorm_ref, conv_ref, gateb_ref, mnorm_ref, lane, everything,
                                 bsz=bsz, n_pieces=n_pieces, rows=rows)
        shared = carry.out
    while pending:
        for gen in list(pending):
            try:
                next(gen)
            except StopIteration:
                pending.remove(gen)


def _chunk_mixers(ci, carry, z_ref, o_ref, negmask_ref, tri, shift_ref, wgate_ref, bgate_ref, gnorm_ref,
                  lb_ref, hnorm_ref, conv_ref, gateb_ref, mnorm_ref, lane, everything, *, bsz, n_pieces, rows):
    def load(off, width):
        return jnp.concatenate([z_ref[b, ci, i, :, off:off + width] for i in range(n_pieces) for b in range(bsz)],
                               axis=0)

    def store(tile_idx, val):
        val = val.astype(o_ref.dtype)
        for i in range(n_pieces):
            for b in range(bsz):
                r0 = (i * bsz + b) * PIECE
                o_ref[b, ci, i, :, tile_idx * LANE:(tile_idx + 1) * LANE] = val[r0:r0 + PIECE]

    small = load(SMALL, LANE)

    def gla_group():
        logits = _dot(small, wgate_ref[...]) + bgate_ref[...]
        g = _log2_sigmoid(logits) * (1.0 / GLA_TAU)
        q = load(OFF["gq"], GLA_QK_PAD) * (GLA_DK ** -0.5)
        k = load(OFF["gk"], GLA_QK_PAD)
        outs = yield from _decayed_attention(q, k, g, lambda p: load(OFF["gv"] + p * LANE, LANE), carry, "gla",
                                             tri, bsz, heads=GLA_HEADS, dk=GLA_DK, n_groups=4)
        for p in range(GLA_HEADS // 2):
            o = outs[p]
            o = o * lax.rsqrt(_head_sums(o * o, lane) + HEAD_DIM * EPS) * gnorm_ref[...]
            o = o * _silu(load(OFF["gout"] + p * LANE, LANE))
            store(p, o)
            yield

    def hgrn_group():
        lb = lb_ref[...]
        f = load(OFF["hf"], HGRN_W)
        t = jnp.exp2(jnp.abs(f) * (-LOG2E))
        r = 1.0 / (1.0 + t)
        pos = f > 0.0
        fg = lb + (1.0 - lb) * (jnp.where(pos, 1.0, t) * r)
        g = jnp.maximum(jnp.log2(fg), jnp.log2(1.0 - lb) + (jnp.minimum(f, 0.0) * LOG2E - 1.0))
        k = (1.0 - lb) * (jnp.where(pos, t, 1.0) * r)
        q = _silu(load(OFF["hq"], HGRN_W))
        yield
        outs = yield from _decayed_attention(q, k, g, lambda p: load(OFF["hi"] + p * LANE, LANE), carry, "hgrn",
                                             tri, bsz, heads=HGRN_HEADS, dk=HGRN_DK, n_groups=4)
        for p in range(HGRN_HEADS // 2):
            o = outs[p]
            o = o * lax.rsqrt(_head_sums(o * o, lane) + HEAD_DIM * EPS) * hnorm_ref[...]
            o = o * jax.nn.sigmoid(load(OFF["hout"] + p * LANE, LANE))
            store(GLA_W // LANE + p, o)
            yield

    def mlstm_group():
        x_qk = load(OFF["mqk"], 2 * MLSTM_W)
        x_bf = x_qk.astype(BF16)
        x_prev = yield from carry.read("conv", everything)
        delayed = jnp.dot(shift_ref[...], jnp.concatenate([x_bf, x_prev], axis=0),
                          preferred_element_type=F32)
        carry.write("conv", everything, x_bf)
        yield
        acc = conv_ref[CONV_WIDTH - 1:CONV_WIDTH, :] * x_qk
        for tap in range(CONV_WIDTH - 1):
            acc = acc + conv_ref[tap:tap + 1, :] * delayed[tap * rows:(tap + 1) * rows]
        qk = _silu(acc)
        yield
        pre = small + gateb_ref[...]
        gates = jnp.where(lane < MLSTM_HEADS, _log2_sigmoid(pre),
                          jnp.where(lane < 2 * MLSTM_HEADS, pre * LOG2E, 0.0))
        cum = _cumsum_rows(gates, tri)
        row_form = jnp.where(lane < MLSTM_HEADS, cum, gates).T
        lane_batch = (lax.broadcasted_iota(jnp.int32, (1, rows), 1) // PIECE) % bsz
        row_batch = (lax.broadcasted_iota(jnp.int32, (rows, 1), 0) // PIECE) % bsz
        ones_tile = jnp.ones((rows, LANE), BF16)
        yield
        for p in range(MLSTM_HEADS // 2):
            q_pair = qk[:, p * LANE:(p + 1) * LANE]
            k_pair = qk[:, MLSTM_W + p * LANE:MLSTM_W + (p + 1) * LANE] * (HEAD_DIM ** -0.5)
            kt_pair = k_pair.T
            v2 = jnp.concatenate([load(OFF["mv"] + p * LANE, LANE).astype(BF16), ones_tile], axis=1)
            y_heads = []
            for hh in range(2):
                h = 2 * p + hh
                own = (lane >= hh * HEAD_DIM) & (lane < (hh + 1) * HEAD_DIM)
                b_row = row_form[h:h + 1, :]
                c_row = row_form[MLSTM_HEADS + h:MLSTM_HEADS + h + 1, :] - b_row
                b_col = jnp.sum(jnp.where(lane == h, cum, 0.0), axis=-1, keepdims=True)
                m_prev, c_prev = [], []
                for bb in range(bsz):
                    m_prev.append((yield from carry.read("m", (slice(bb, bb + 1), slice(h, h + 1)))))
                    c_prev.append((yield from carry.read("mlstm", (h, bb))))
                m_prev_col = _per_batch_select(m_prev, row_batch, bsz)
                c_masked = c_row + negmask_ref[...]
                mu = jnp.maximum(m_prev_col, jnp.max(c_masked, axis=-1, keepdims=True))
                q_own = jnp.where(own, q_pair, 0.0)
                prob = jnp.exp2(c_masked - mu) * _dot_nt(q_own, k_pair)
                res = _dot(prob, v2)
                qo = _pieces(q_own, bsz)
                inter_b = []
                for bb in range(bsz):
                    lhs = jnp.concatenate([qo[i][bb] for i in range(n_pieces)], axis=0)
                    inter_b.append(_dot(lhs, c_prev[bb]))
                inter = jnp.concatenate([inter_b[bb][i * PIECE:(i + 1) * PIECE]
                                         for i in range(n_pieces) for bb in range(bsz)], axis=0)
                res = res + jnp.exp2(m_prev_col - mu) * inter
                den = jnp.maximum(jnp.abs(res[:, LANE:]), jnp.exp2(-(b_col + mu)))
                y_heads.append(res[:, :LANE] / den)
                yield
                b_end = [b_col[rows - (bsz - bb) * PIECE + PIECE - 1:rows - (bsz - bb) * PIECE + PIECE, :]
                         for bb in range(bsz)]
                w_row = _per_batch_select(b_end, lane_batch, bsz) + c_row
                m_loc = [jnp.max(jnp.where(lane_batch == bb, w_row, NEG), axis=-1, keepdims=True)
                         for bb in range(bsz)]
                m_new = [jnp.maximum(b_end[bb] + m_prev[bb], m_loc[bb]) for bb in range(bsz)]
                ks_t = (kt_pair * jnp.exp2(w_row - _per_batch_select(m_new, lane_batch, bsz))).astype(BF16)
                lhs = jnp.concatenate([jnp.where(lane_batch == bb, ks_t, jnp.zeros_like(ks_t)) for bb in range(bsz)],
                                      axis=0)
                upd = _dot(lhs, v2)
                for bb in range(bsz):
                    s_old = jnp.exp2(b_end[bb] + m_prev[bb] - m_new[bb])
                    carry.write("mlstm", (h, bb), s_old * c_prev[bb] + upd[bb * LANE:(bb + 1) * LANE])
                    carry.write("m", (slice(bb, bb + 1), slice(h, h + 1)), m_new[bb])
                yield
            y = _halves(lane, y_heads[0], y_heads[1])
            mu_h = _head_sums(y, lane) * (1.0 / HEAD_DIM)
            yc = y - mu_h
            sq = _head_sums(yc * yc, lane)
            y = yc * lax.rsqrt(sq + HEAD_DIM * EPS) * mnorm_ref[:, p * LANE:(p + 1) * LANE]
            y = y * jax.nn.sigmoid(load(OFF["mout"] + p * LANE, LANE))
            store((GLA_W + HGRN_W) // LANE + p, y)
            yield

    return [gla_group(), hgrn_group(), mlstm_group()]


def _mixers(z, params, layer):
    bsz, s, nz = z.shape
    d_out = GLA_W + HGRN_W + MLSTM_W
    n_chunks = s // CHUNK
    n_pieces = CHUNK // PIECE
    rows = bsz * CHUNK
    mask = _stack_masks(bsz)
    consts = [jnp.asarray((1.0 - mask) * NEG), jnp.asarray(mask, dtype=BF16),
              jnp.asarray(_shift_matrix(bsz), dtype=BF16)]
    whole = lambda a: pl.BlockSpec(a.shape, lambda c: (0,) * a.ndim, pipeline_mode=pl.Buffered(1))
    out = pl.pallas_call(
        functools.partial(_mixer_kernel, bsz=bsz),
        grid=(n_chunks // CHUNKS_PER_STEP,),
        in_specs=([pl.BlockSpec((bsz, CHUNKS_PER_STEP, n_pieces, PIECE, nz), lambda c: (0, c, 0, 0, 0))]
                  + [whole(a) for a in consts] + [_layer_block(a, layer) for a in params]),
        out_specs=pl.BlockSpec((bsz, CHUNKS_PER_STEP, n_pieces, PIECE, d_out), lambda c: (0, c, 0, 0, 0)),
        out_shape=jax.ShapeDtypeStruct((bsz, n_chunks, n_pieces, PIECE, d_out), BF16),
        scratch_shapes=[pltpu.VMEM((GLA_HEADS // 2, bsz, LANE, LANE), F32),
                        pltpu.VMEM((HGRN_HEADS // 2, bsz, LANE, LANE), F32),
                        pltpu.VMEM((MLSTM_HEADS, bsz, LANE, 2 * LANE), F32),
                        pltpu.VMEM((8, LANE), F32),
                        pltpu.VMEM((rows, 2 * MLSTM_W), BF16)],
        compiler_params=pltpu.CompilerParams(dimension_semantics=("arbitrary",),
                                             vmem_limit_bytes=48 * MIB),
        name="token_mixers",
    )(z.reshape(bsz, n_chunks, n_pieces, PIECE, nz), *consts, *params)
    return out.reshape(bsz, s, d_out)


def _out_mlp_kernel(mixed_ref, x_ref, gate1_ref, shift_ref, scale_ref, gate2_ref, gain_ref,
                    wout_ref, w1_ref, w2_ref, fnorm_ref, o_ref, *, final):
    part = x_ref.shape[0] // ROW_PARTS
    for h in range(ROW_PARTS):
        rows = slice(h * part, (h + 1) * part)
        x = x_ref[rows, :] + gate1_ref[...] * jnp.dot(mixed_ref[rows, :], wout_ref[...], preferred_element_type=F32)
        y = _rms(x) * gain_ref[...]
        y = y * (1.0 + scale_ref[...]) + shift_ref[...]
        u = jnp.dot(y.astype(BF16), w1_ref[...], preferred_element_type=F32)
        u = jnp.square(jnp.maximum(u, 0.0))
        x = x + gate2_ref[...] * jnp.dot(u.astype(BF16), w2_ref[...], preferred_element_type=F32)
        if final:
            x = _rms(x) * fnorm_ref[...]
        o_ref[rows, :] = x


def _out_mlp(mixed, x, gate1, shift, scale, gate2, gains, w_out, w1, w2, fnorm, layer, *, final, tm=1024):
    bsz, s, d = x.shape
    tok = pl.BlockSpec((None, tm, d), lambda b, i: (b, i, 0))
    mod = pl.BlockSpec((None, 1, d), lambda b, i: (b, 0, 0))
    whole = lambda a: pl.BlockSpec(a.shape, lambda b, i: (0,) * a.ndim, pipeline_mode=pl.Buffered(1))
    return pl.pallas_call(
        functools.partial(_out_mlp_kernel, final=final),
        grid=(bsz, s // tm),
        in_specs=[tok, tok, mod, mod, mod, mod, _layer_block(gains, layer),
                  whole(w_out), whole(w1), whole(w2), whole(fnorm)],
        out_specs=tok,
        out_shape=jax.ShapeDtypeStruct((bsz, s, d), F32),
        compiler_params=pltpu.CompilerParams(dimension_semantics=("arbitrary", "arbitrary"),
                                             vmem_limit_bytes=60 * MIB),
        name="out_proj_mlp",
    )(mixed, x, gate1, shift, scale, gate2, gains, w_out, w1, w2, fnorm)


def kernel(x, c, w_ada, b_ada, norm_mix, norm_mlp, w_in, gla_w_gate, gla_b_gate, gla_norm, hgrn_lb_logits,
           hgrn_norm, mlstm_conv, mlstm_gate_b, mlstm_norm, w_out, w_ff1, w_ff2, final_norm):
    depth, d, _ = w_ada.shape
    bsz = x.shape[0]
    mod = _adaln(c.astype(F32), w_ada.astype(F32), b_ada.astype(F32))
    mod = mod.reshape(depth, bsz, 6, 1, d)
    lower = _lower_bounds(hgrn_lb_logits.astype(F32))
    w_in_p = _permute_in_proj(w_in)
    gate_b = jnp.concatenate([mlstm_gate_b[:, MLSTM_HEADS:], mlstm_gate_b[:, :MLSTM_HEADS],
                              jnp.zeros((depth, LANE - 2 * MLSTM_HEADS), F32)], axis=-1)
    wgate = jnp.zeros((depth, LANE, GLA_QK_PAD), F32)
    wgate = wgate.at[:, GLOW_LANE:GLOW_LANE + GLA_RANK, :GLA_QK].set(gla_w_gate).astype(BF16)
    row = lambda a: a.reshape(depth, 1, -1).astype(F32)
    mixer_params = [wgate, row(jnp.pad(gla_b_gate, ((0, 0), (0, GLA_QK_PAD - GLA_QK)))),
                    row(jnp.concatenate([gla_norm, gla_norm], axis=-1) * HEAD_DIM ** 0.5),
                    row(lower), row(jnp.concatenate([hgrn_norm, hgrn_norm], axis=-1) * HEAD_DIM ** 0.5),
                    mlstm_conv.astype(F32), row(gate_b), row(mlstm_norm * HEAD_DIM ** 0.5)]
    late_weights = [w_out.astype(F32), w_ff1.astype(F32), w_ff2.astype(F32)]
    gains_mix, gains_mlp = row(norm_mix), row(norm_mlp)
    fnorm = final_norm.reshape(1, d).astype(F32)
    for l in range(depth):
        shift1, scale1, gate1, shift2, scale2, gate2 = (mod[l, :, j] for j in range(6))
        z, (w_out_b, w1_b, w2_b) = _in_proj(x, shift1, scale1, gains_mix, w_in_p, l, late_weights)
        mixed = _mixers(z, mixer_params, l)
        x = _out_mlp(mixed, x, gate1, shift2, scale2, gate2, gains_mlp, w_out_b, w1_b, w2_b, fnorm, l,
                     final=(l == depth - 1))
    return x
```

```python
import functools

import numpy as np
import jax
import jax.numpy as jnp
from jax import lax
from jax.experimental import pallas as pl
from jax.experimental.pallas import tpu as pltpu

F32 = jnp.float32
BF16 = jnp.bfloat16

HEAD_DIM = 64
GLA_HEADS = 6
GLA_DK = 32
GLA_RANK = 16
GLA_TAU = 16.0
HGRN_HEADS = 6
HGRN_DK = 64
MLSTM_HEADS = 4
CONV_WIDTH = 4
EPS = 1e-6
CHUNK = 64
CHUNKS_PER_STEP = 2
PIECE = 16
LANE = 128
NEG = -1e30
MAX_EXP2 = 115.0
LOG2E = 1.4426950408889634
MIB = 1024 * 1024
V7X_VMEM_BYTES = 64 * MIB
SCOPED_VMEM_BYTES = V7X_VMEM_BYTES - 4 * MIB
MIXER_VMEM_BYTES = 48 * MIB
ROW_TILE = 1024
IN_PROJ_ROW_PARTS = 2

GLA_QK = GLA_HEADS * GLA_DK
GLA_W = GLA_HEADS * HEAD_DIM
HGRN_W = HGRN_HEADS * HEAD_DIM
MLSTM_W = MLSTM_HEADS * HEAD_DIM
GLA_QK_PAD = 2 * LANE

_REF_SEGMENTS = (("gq", GLA_QK), ("gk", GLA_QK), ("gv", GLA_W), ("glow", GLA_RANK), ("gout", GLA_W),
                 ("hq", HGRN_W), ("hf", HGRN_W), ("hi", HGRN_W), ("hout", HGRN_W),
                 ("mqk", 2 * MLSTM_W), ("mv", MLSTM_W), ("mi", MLSTM_HEADS), ("mf", MLSTM_HEADS),
                 ("mout", MLSTM_W))
_MY_ORDER = (("gq", 0), ("pad", GLA_QK_PAD - GLA_QK), ("gk", 0), ("pad", GLA_QK_PAD - GLA_QK),
             ("gv", 0), ("gout", 0), ("hq", 0), ("hf", 0), ("hi", 0), ("hout", 0),
             ("mqk", 0), ("mv", 0), ("mout", 0), ("mf", 0), ("mi", 0), ("glow", 0))


def _layout():
    ref_off, off = {}, 0
    for name, width in _REF_SEGMENTS:
        ref_off[name] = (off, width)
        off += width
    my_off, off = {}, 0
    for name, pad in _MY_ORDER:
        if name == "pad":
            off += pad
        else:
            my_off[name] = off
            off += ref_off[name][1]
    padded = -(-off // LANE) * LANE
    return ref_off, my_off, off, padded


_REF_OFF, OFF, _NZ_USED, NZ = _layout()
SMALL = OFF["mf"]
GLOW_LANE = OFF["glow"] - SMALL
assert SMALL % LANE == 0 and OFF["mi"] == SMALL + MLSTM_HEADS and GLOW_LANE == 2 * MLSTM_HEADS
assert all(OFF[n] % LANE == 0 for n in ("gq", "gk", "gv", "gout", "hq", "hf", "hi", "hout", "mqk", "mv", "mout"))


def _permute_kernel(wt_ref, o_ref, rows_scr):
    off = 0
    for name, pad in _MY_ORDER:
        if name == "pad":
            rows_scr[off:off + pad, :] = jnp.zeros((pad, rows_scr.shape[1]), F32)
            off += pad
        else:
            lo, width = _REF_OFF[name]
            rows_scr[off:off + width, :] = wt_ref[lo:lo + width, :]
            off += width
    rows_scr[off:, :] = jnp.zeros((rows_scr.shape[0] - off, rows_scr.shape[1]), F32)
    for t in range(rows_scr.shape[0] // LANE):
        o_ref[:, t * LANE:(t + 1) * LANE] = rows_scr[t * LANE:(t + 1) * LANE, :].T.astype(o_ref.dtype)


def _permute_in_proj(w_in, *, tc=256):
    depth, d, d_in = w_in.shape
    return pl.pallas_call(
        _permute_kernel,
        grid=(depth, d // tc),
        in_specs=[pl.BlockSpec((None, d_in, tc), lambda l, i: (l, 0, i))],
        out_specs=pl.BlockSpec((None, tc, NZ), lambda l, i: (l, i, 0)),
        out_shape=jax.ShapeDtypeStruct((depth, d, NZ), BF16),
        scratch_shapes=[pltpu.VMEM((NZ, tc), F32)],
        name="permute_in_proj_weight",
    )(jnp.swapaxes(w_in, 1, 2))


def _silu(x):
    return x * jax.nn.sigmoid(x)


def _log2_sigmoid(x):
    return jnp.minimum(x, 0.0) * LOG2E - jnp.log2(1.0 + jnp.exp2(jnp.abs(x) * (-LOG2E)))


def _dot(a, b):
    return jnp.dot(a.astype(BF16), b.astype(BF16), preferred_element_type=F32)


def _dot_nt(a, b):
    return lax.dot_general(a.astype(BF16), b.astype(BF16), (((1,), (1,)), ((), ())),
                           preferred_element_type=F32)


def _cumsum_rows(g, tri):
    hi = g.astype(BF16)
    lo = (g - hi.astype(F32)).astype(BF16)
    return (jnp.dot(tri, hi, preferred_element_type=F32) + jnp.dot(tri, lo, preferred_element_type=F32))


def _adaln_kernel(c_ref, w_ref, b_ref, o_ref):
    cond = _silu(c_ref[...])
    o_ref[...] = _dot(cond, w_ref[...]) + b_ref[...]


def _adaln(c, w_ada, b_ada, *, tn=2048):
    depth, d, n = w_ada.shape
    bsz = c.shape[0]
    return pl.pallas_call(
        _adaln_kernel,
        grid=(depth, n // tn),
        in_specs=[pl.BlockSpec((bsz, d), lambda l, j: (0, 0)),
                  pl.BlockSpec((None, d, tn), lambda l, j: (l, 0, j)),
                  pl.BlockSpec((None, 1, tn), lambda l, j: (l, 0, j))],
        out_specs=pl.BlockSpec((None, bsz, tn), lambda l, j: (l, 0, j)),
        out_shape=jax.ShapeDtypeStruct((depth, bsz, n), F32),
        name="adaln_modulation",
    )(c, w_ada, b_ada.reshape(depth, 1, n))


def _lower_bound_kernel(logit_ref, o_ref):
    x = logit_ref[...]
    e = jnp.exp(x - jnp.max(x, axis=0, keepdims=True))
    p = e / jnp.sum(e, axis=0, keepdims=True)
    depth = x.shape[0]
    acc = p[0:1]
    rows = [acc]
    for l in range(1, depth):
        acc = acc + p[l:l + 1]
        rows.append(acc)
    cum = jnp.concatenate(rows, axis=0)
    o_ref[...] = cum - cum[0:1]


def _lower_bounds(lb_logits):
    return pl.pallas_call(
        _lower_bound_kernel,
        out_shape=jax.ShapeDtypeStruct(lb_logits.shape, F32),
        name="hgrn_lower_bounds",
    )(lb_logits)


def _rms(x):
    return x * lax.rsqrt(jnp.mean(x * x, axis=-1, keepdims=True) + EPS)


def _layer_block(arr, layer):
    tail = arr.shape[1:]
    return pl.BlockSpec((None,) + tail, lambda *_: (layer,) + (0,) * len(tail), pipeline_mode=pl.Buffered(1))


def _in_proj_kernel(x_ref, shift_ref, gscale_ref, w_ref, *rest):
    n_cast = (len(rest) - 1) // 2
    z_ref = rest[n_cast]
    part = x_ref.shape[0] // IN_PROJ_ROW_PARTS
    ys = []
    for h in range(IN_PROJ_ROW_PARTS):
        ys.append((_rms(x_ref[h * part:(h + 1) * part, :]) * gscale_ref[...] + shift_ref[...]).astype(BF16))
    for h in range(IN_PROJ_ROW_PARTS):
        z_ref[h * part:(h + 1) * part, :] = jnp.dot(ys[h], w_ref[...], preferred_element_type=F32)
    for src, dst in zip(rest[:n_cast], rest[n_cast + 1:]):
        dst[...] = src[...].astype(dst.dtype)


def _in_proj(x, shift, gscale, w, layer, cast, *, tm=ROW_TILE):
    bsz, s, d = x.shape
    nz = w.shape[-1]
    n_i = s // tm
    steps = bsz * n_i
    slab = lambda a: a.shape[1] // steps
    outs = pl.pallas_call(
        _in_proj_kernel,
        grid=(bsz, n_i),
        in_specs=([pl.BlockSpec((None, tm, d), lambda b, i: (b, i, 0)),
                   pl.BlockSpec((None, 1, d), lambda b, i: (b, 0, 0)),
                   pl.BlockSpec((None, 1, d), lambda b, i: (b, 0, 0)),
                   _layer_block(w, layer)]
                  + [pl.BlockSpec((None, slab(a), a.shape[2]), lambda b, i: (layer, b * n_i + i, 0)) for a in cast]),
        out_specs=([pl.BlockSpec((None, tm, nz), lambda b, i: (b, i, 0))]
                   + [pl.BlockSpec((slab(a), a.shape[2]), lambda b, i: (b * n_i + i, 0)) for a in cast]),
        out_shape=([jax.ShapeDtypeStruct((bsz, s, nz), F32)]
                   + [jax.ShapeDtypeStruct(a.shape[1:], BF16) for a in cast]),
        compiler_params=pltpu.CompilerParams(dimension_semantics=("arbitrary", "arbitrary"),
                                             vmem_limit_bytes=SCOPED_VMEM_BYTES),
        name="norm_in_proj",
    )(x, shift, gscale, w, *cast)
    return outs[0], outs[1:]


def _stack_masks(bsz):
    r = np.arange(bsz * CHUNK)
    batch = (r // PIECE) % bsz
    pos = (r // (PIECE * bsz)) * PIECE + r % PIECE
    return ((batch[:, None] == batch[None, :]) & (pos[None, :] <= pos[:, None])).astype(np.float32)


def _shift_matrix(bsz):
    rows = bsz * CHUNK
    r = np.arange(rows)
    batch = (r // PIECE) % bsz
    pos = (r // (PIECE * bsz)) * PIECE + r % PIECE
    same = batch[:, None] == batch[None, :]
    blocks = []
    for d in range(CONV_WIDTH - 1, 0, -1):
        cur = same & (pos[None, :] == pos[:, None] - d)
        prev = same & (pos[None, :] == pos[:, None] - d + CHUNK)
        blocks.append(np.concatenate([cur, prev], axis=1))
    return np.concatenate(blocks, axis=0).astype(np.float32)


def _pieces(x, bsz):
    n_groups = x.shape[0] // (PIECE * bsz)
    return [[x[(i * bsz + b) * PIECE:(i * bsz + b + 1) * PIECE] for b in range(bsz)] for i in range(n_groups)]


def _per_batch_rows(rows, bsz, reps):
    slab = jnp.concatenate([jnp.broadcast_to(rows[b], (PIECE, rows[b].shape[-1])) for b in range(bsz)], axis=0)
    return jnp.concatenate([slab] * reps, axis=0) if reps > 1 else slab


def _per_batch_select(vals, batch_ids, bsz):
    out = vals[0]
    for b in range(1, bsz):
        out = jnp.where(batch_ids == b, vals[b], out)
    return out


def _halves(lane, even, odd):
    return jnp.where(lane < HEAD_DIM, even, odd)


def _head_sums(x, lane):
    lo = jnp.sum(jnp.where(lane < HEAD_DIM, x, 0.0), axis=-1, keepdims=True)
    hi = jnp.sum(jnp.where(lane < HEAD_DIM, 0.0, x), axis=-1, keepdims=True)
    return _halves(lane, lo, hi)


class _Carry:
    def __init__(self, refs, shared, last):
        self.refs, self.shared, self.last = refs, shared, last
        self.out = {}

    def read(self, name, idx):
        if self.shared is None:
            return self.refs[name][idx]
        while (name, idx) not in self.shared:
            yield
        return self.shared[(name, idx)]

    def write(self, name, idx, val):
        if self.last:
            self.refs[name][idx] = val
        else:
            self.out[(name, idx)] = val


def _decayed_attention(q, k, g, v_tile, carry, name, tri, bsz, *, heads, dk, n_groups):
    rows, wk = q.shape
    glen = CHUNK // n_groups
    gp = glen // PIECE
    grows = rows // n_groups
    b = _cumsum_rows(g, tri)
    bp = _pieces(b, bsz)

    def ref_rows(gi):
        mid = gi * glen + glen // 2
        return _per_batch_rows([bp[mid // PIECE][bb][mid % PIECE:mid % PIECE + 1] for bb in range(bsz)], bsz, gp)

    refs = [ref_rows(gi) for gi in range(n_groups)]
    q_in = q * jnp.exp2(jnp.minimum(b - jnp.concatenate(refs, axis=0), MAX_EXP2))
    q_state = q * jnp.exp2(b)
    last = [bp[-1][bb][PIECE - 1:PIECE] for bb in range(bsz)]
    k_state = (k * jnp.exp2(_per_batch_rows(last, bsz, rows // (bsz * PIECE)) - b)).astype(BF16)
    yield
    k_in = []
    for gi in range(n_groups):
        lo, end = gi * grows, (gi + 1) * grows
        own = k[lo:end] * jnp.exp2(jnp.minimum(refs[gi] - b[lo:end], MAX_EXP2))
        if gi:
            ref = jnp.concatenate([refs[gi]] * gi, axis=0) if gi > 1 else refs[gi]
            own = jnp.concatenate([k[:lo] * jnp.exp2(ref - b[:lo]), own], axis=0)
        k_in.append(own.astype(BF16))
    yield

    lane = lax.broadcasted_iota(jnp.int32, (1, LANE), 1)
    heads_per_tile = LANE // dk
    n_pairs = heads // 2
    probs = [None] * heads
    for kt in range(-(-heads // heads_per_tile)):
        tile_heads = list(range(kt * heads_per_tile, min(heads, (kt + 1) * heads_per_tile)))
        tl = slice(kt * LANE, (kt + 1) * LANE)
        per_head = {h: [] for h in tile_heads}
        for gi in range(n_groups):
            end = (gi + 1) * grows
            qg = q_in[gi * grows:end, tl]
            lhs = jnp.concatenate(
                [jnp.where((lane >= (h % heads_per_tile) * dk) & (lane < (h % heads_per_tile + 1) * dk), qg, 0.0)
                 for h in tile_heads], axis=0).astype(BF16)
            sc = _dot_nt(lhs, k_in[gi][:, tl])
            keep = tri[gi * grows:end, :end] > 0
            for n, h in enumerate(tile_heads):
                blk = sc[n * grows:(n + 1) * grows].astype(BF16)
                blk = jnp.where(keep, blk, jnp.zeros_like(blk))
                if end < rows:
                    blk = jnp.concatenate([blk, jnp.zeros((grows, rows - end), BF16)], axis=1)
                per_head[h].append(blk)
        for h in tile_heads:
            probs[h] = jnp.concatenate(per_head[h], axis=0)
        yield

    lane_batch = (lax.broadcasted_iota(jnp.int32, (1, rows), 1) // PIECE) % bsz
    row_i = lax.broadcasted_iota(jnp.int32, (LANE, LANE), 0)
    col_i = lax.broadcasted_iota(jnp.int32, (LANE, LANE), 1)
    outs = []
    for p in range(n_pairs):
        kt = (2 * p) // heads_per_tile
        tl = slice(kt * LANE, (kt + 1) * LANE)
        v = v_tile(p)
        vb = v.astype(BF16)
        intra = _halves(lane, _dot(probs[2 * p], vb), _dot(probs[2 * p + 1], vb))
        qs = _pieces(q_state[:, tl], bsz)
        inter_b, states = [], []
        for bb in range(bsz):
            lhs = jnp.concatenate([qs[i][bb] for i in range(len(qs))], axis=0)
            states.append((yield from carry.read(name, (p, bb))))
            inter_b.append(_dot_nt(lhs, states[bb]))
        inter = jnp.concatenate([inter_b[bb][i * PIECE:(i + 1) * PIECE]
                                 for i in range(len(qs)) for bb in range(bsz)], axis=0)
        outs.append(intra + inter)
        yield
        vt = v.T.astype(BF16)
        lhs = jnp.concatenate([jnp.where(lane_batch == bb, vt, jnp.zeros_like(vt)) for bb in range(bsz)], axis=0)
        upd = _dot(lhs, k_state[:, tl])
        same_head = (2 * p + row_i // HEAD_DIM) == (kt * heads_per_tile + col_i // dk)
        for bb in range(bsz):
            decay = jnp.exp2(last[bb][:, tl])
            carry.write(name, (p, bb),
                        states[bb] * decay + jnp.where(same_head, upd[bb * LANE:(bb + 1) * LANE], 0.0))
        yield
    return outs


def _mixer_kernel(z_ref, negmask_ref, tri_ref, shift_ref, wgate_ref, bgate_ref, gnorm_ref, lb_ref, hnorm_ref,
                  conv_ref, gateb_ref, mnorm_ref, o_ref, gla_state, hgrn_state, mlstm_state, mlstm_m, conv_carry,
                  *, bsz):
    @pl.when(pl.program_id(0) == 0)
    def _():
        gla_state[...] = jnp.zeros_like(gla_state)
        hgrn_state[...] = jnp.zeros_like(hgrn_state)
        mlstm_state[...] = jnp.zeros_like(mlstm_state)
        mlstm_m[...] = jnp.zeros_like(mlstm_m)
        conv_carry[...] = jnp.zeros_like(conv_carry)

    n_pieces = CHUNK // PIECE
    rows = bsz * CHUNK
    tri = tri_ref[...]
    lane = lax.broadcasted_iota(jnp.int32, (1, LANE), 1)
    refs = {"gla": gla_state, "hgrn": hgrn_state, "mlstm": mlstm_state, "m": mlstm_m, "conv": conv_carry}
    everything = (Ellipsis,)

    pending, shared = [], None
    for ci in range(CHUNKS_PER_STEP):
        carry = _Carry(refs, shared, last=(ci == CHUNKS_PER_STEP - 1))
        pending += _chunk_mixers(ci, carry, z_ref, o_ref, negmask_ref, tri, shift_ref, wgate_ref, bgate_ref,
                                 gnorm_ref, lb_ref, hnorm_ref, conv_ref, gateb_ref, mnorm_ref, lane, everything,
                                 bsz=bsz, n_pieces=n_pieces, rows=rows)
        shared = carry.out
    while pending:
        for gen in list(pending):
            try:
                next(gen)
            except StopIteration:
                pending.remove(gen)


def _chunk_mixers(ci, carry, z_ref, o_ref, negmask_ref, tri, shift_ref, wgate_ref, bgate_ref, gnorm_ref,
                  lb_ref, hnorm_ref, conv_ref, gateb_ref, mnorm_ref, lane, everything, *, bsz, n_pieces, rows):
    def load(off, width):
        return jnp.concatenate([z_ref[b, ci, i, :, off:off + width] for i in range(n_pieces) for b in range(bsz)],
                               axis=0)

    def store(tile_idx, val):
        val = val.astype(o_ref.dtype)
        for i in range(n_pieces):
            for b in range(bsz):
                r0 = (i * bsz + b) * PIECE
                o_ref[b, ci, i, :, tile_idx * LANE:(tile_idx + 1) * LANE] = val[r0:r0 + PIECE]

    small = load(SMALL, LANE)

    def gla_group():
        logits = _dot(small, wgate_ref[...]) + bgate_ref[...]
        g = _log2_sigmoid(logits) * (1.0 / GLA_TAU)
        q = load(OFF["gq"], GLA_QK_PAD) * (GLA_DK ** -0.5)
        k = load(OFF["gk"], GLA_QK_PAD)
        outs = yield from _decayed_attention(q, k, g, lambda p: load(OFF["gv"] + p * LANE, LANE), carry, "gla",
                                             tri, bsz, heads=GLA_HEADS, dk=GLA_DK, n_groups=4)
        for p in range(GLA_HEADS // 2):
            o = outs[p]
            o = o * lax.rsqrt(_head_sums(o * o, lane) + HEAD_DIM * EPS) * gnorm_ref[...]
            o = o * _silu(load(OFF["gout"] + p * LANE, LANE))
            store(p, o)
            yield

    def hgrn_group():
        lb = lb_ref[...]
        f = load(OFF["hf"], HGRN_W)
        t = jnp.exp2(jnp.abs(f) * (-LOG2E))
        r = 1.0 / (1.0 + t)
        pos = f > 0.0
        fg = lb + (1.0 - lb) * (jnp.where(pos, 1.0, t) * r)
        g = jnp.maximum(jnp.log2(fg), jnp.log2(1.0 - lb) + (jnp.minimum(f, 0.0) * LOG2E - 1.0))
        k = (1.0 - lb) * (jnp.where(pos, t, 1.0) * r)
        q = _silu(load(OFF["hq"], HGRN_W))
        yield
        outs = yield from _decayed_attention(q, k, g, lambda p: load(OFF["hi"] + p * LANE, LANE), carry, "hgrn",
                                             tri, bsz, heads=HGRN_HEADS, dk=HGRN_DK, n_groups=4)
        for p in range(HGRN_HEADS // 2):
            o = outs[p]
            o = o * lax.rsqrt(_head_sums(o * o, lane) + HEAD_DIM * EPS) * hnorm_ref[...]
            o = o * jax.nn.sigmoid(load(OFF["hout"] + p * LANE, LANE))
            store(GLA_W // LANE + p, o)
            yield

    def mlstm_group():
        x_qk = load(OFF["mqk"], 2 * MLSTM_W)
        x_bf = x_qk.astype(BF16)
        x_prev = yield from carry.read("conv", everything)
        delayed = jnp.dot(shift_ref[...], jnp.concatenate([x_bf, x_prev], axis=0),
                          preferred_element_type=F32)
        carry.write("conv", everything, x_bf)
        yield
        acc = conv_ref[CONV_WIDTH - 1:CONV_WIDTH, :] * x_qk
        for tap in range(CONV_WIDTH - 1):
            acc = acc + conv_ref[tap:tap + 1, :] * delayed[tap * rows:(tap + 1) * rows]
        qk = _silu(acc)
        yield
        pre = small + gateb_ref[...]
        gates = jnp.where(lane < MLSTM_HEADS, _log2_sigmoid(pre),
                          jnp.where(lane < 2 * MLSTM_HEADS, pre * LOG2E, 0.0))
        cum = _cumsum_rows(gates, tri)
        row_form = jnp.where(lane < MLSTM_HEADS, cum, gates).T
        lane_batch = (lax.broadcasted_iota(jnp.int32, (1, rows), 1) // PIECE) % bsz
        row_batch = (lax.broadcasted_iota(jnp.int32, (rows, 1), 0) // PIECE) % bsz
        ones_tile = jnp.ones((rows, LANE), BF16)
        yield
        for p in range(MLSTM_HEADS // 2):
            q_pair = qk[:, p * LANE:(p + 1) * LANE]
            k_pair = qk[:, MLSTM_W + p * LANE:MLSTM_W + (p + 1) * LANE] * (HEAD_DIM ** -0.5)
            kt_pair = k_pair.T
            v2 = jnp.concatenate([load(OFF["mv"] + p * LANE, LANE).astype(BF16), ones_tile], axis=1)
            y_heads = []
            for hh in range(2):
                h = 2 * p + hh
                own = (lane >= hh * HEAD_DIM) & (lane < (hh + 1) * HEAD_DIM)
                b_row = row_form[h:h + 1, :]
                c_row = row_form[MLSTM_HEADS + h:MLSTM_HEADS + h + 1, :] - b_row
                b_col = jnp.sum(jnp.where(lane == h, cum, 0.0), axis=-1, keepdims=True)
                m_prev, c_prev = [], []
                for bb in range(bsz):
                    m_prev.append((yield from carry.read("m", (slice(bb, bb + 1), slice(h, h + 1)))))
                    c_prev.append((yield from carry.read("mlstm", (h, bb))))
                m_prev_col = _per_batch_select(m_prev, row_batch, bsz)
                c_masked = c_row + negmask_ref[...]
                mu = jnp.maximum(m_prev_col, jnp.max(c_masked, axis=-1, keepdims=True))
                q_own = jnp.where(own, q_pair, 0.0)
                prob = jnp.exp2(c_masked - mu) * _dot_nt(q_own, k_pair)
                res = _dot(prob, v2)
                qo = _pieces(q_own, bsz)
                inter_b = []
                for bb in range(bsz):
                    lhs = jnp.concatenate([qo[i][bb] for i in range(n_pieces)], axis=0)
                    inter_b.append(_dot(lhs, c_prev[bb]))
                inter = jnp.concatenate([inter_b[bb][i * PIECE:(i + 1) * PIECE]
                                         for i in range(n_pieces) for bb in range(bsz)], axis=0)
                res = res + jnp.exp2(m_prev_col - mu) * inter
                den = jnp.maximum(jnp.abs(res[:, LANE:]), jnp.exp2(-(b_col + mu)))
                y_heads.append(res[:, :LANE] / den)
                yield
                b_end = [b_col[rows - (bsz - bb) * PIECE + PIECE - 1:rows - (bsz - bb) * PIECE + PIECE, :]
                         for bb in range(bsz)]
                w_row = _per_batch_select(b_end, lane_batch, bsz) + c_row
                m_loc = [jnp.max(jnp.where(lane_batch == bb, w_row, NEG), axis=-1, keepdims=True)
                         for bb in range(bsz)]
                m_new = [jnp.maximum(b_end[bb] + m_prev[bb], m_loc[bb]) for bb in range(bsz)]
                ks_t = (kt_pair * jnp.exp2(w_row - _per_batch_select(m_new, lane_batch, bsz))).astype(BF16)
                lhs = jnp.concatenate([jnp.where(lane_batch == bb, ks_t, jnp.zeros_like(ks_t)) for bb in range(bsz)],
                                      axis=0)
                upd = _dot(lhs, v2)
                for bb in range(bsz):
                    s_old = jnp.exp2(b_end[bb] + m_prev[bb] - m_new[bb])
                    carry.write("mlstm", (h, bb), s_old * c_prev[bb] + upd[bb * LANE:(bb + 1) * LANE])
                    carry.write("m", (slice(bb, bb + 1), slice(h, h + 1)), m_new[bb])
                yield
            y = _halves(lane, y_heads[0], y_heads[1])
            mu_h = _head_sums(y, lane) * (1.0 / HEAD_DIM)
            yc = y - mu_h
            sq = _head_sums(yc * yc, lane)
            y = yc * lax.rsqrt(sq + HEAD_DIM * EPS) * mnorm_ref[:, p * LANE:(p + 1) * LANE]
            y = y * jax.nn.sigmoid(load(OFF["mout"] + p * LANE, LANE))
            store((GLA_W + HGRN_W) // LANE + p, y)
            yield

    return [gla_group(), hgrn_group(), mlstm_group()]


def _mixers(z, params, layer):
    bsz, s, nz = z.shape
    d_out = GLA_W + HGRN_W + MLSTM_W
    n_chunks = s // CHUNK
    n_pieces = CHUNK // PIECE
    rows = bsz * CHUNK
    mask = _stack_masks(bsz)
    consts = [jnp.asarray((1.0 - mask) * NEG), jnp.asarray(mask, dtype=BF16),
              jnp.asarray(_shift_matrix(bsz), dtype=BF16)]
    whole = lambda a: pl.BlockSpec(a.shape, lambda c: (0,) * a.ndim, pipeline_mode=pl.Buffered(1))
    out = pl.pallas_call(
        functools.partial(_mixer_kernel, bsz=bsz),
        grid=(n_chunks // CHUNKS_PER_STEP,),
        in_specs=([pl.BlockSpec((bsz, CHUNKS_PER_STEP, n_pieces, PIECE, nz), lambda c: (0, c, 0, 0, 0))]
                  + [whole(a) for a in consts] + [_layer_block(a, layer) for a in params]),
        out_specs=pl.BlockSpec((bsz, CHUNKS_PER_STEP, n_pieces, PIECE, d_out), lambda c: (0, c, 0, 0, 0)),
        out_shape=jax.ShapeDtypeStruct((bsz, n_chunks, n_pieces, PIECE, d_out), BF16),
        scratch_shapes=[pltpu.VMEM((GLA_HEADS // 2, bsz, LANE, LANE), F32),
                        pltpu.VMEM((HGRN_HEADS // 2, bsz, LANE, LANE), F32),
                        pltpu.VMEM((MLSTM_HEADS, bsz, LANE, 2 * LANE), F32),
                        pltpu.VMEM((8, LANE), F32),
                        pltpu.VMEM((rows, 2 * MLSTM_W), BF16)],
        compiler_params=pltpu.CompilerParams(dimension_semantics=("arbitrary",),
                                             vmem_limit_bytes=MIXER_VMEM_BYTES),
        name="token_mixers",
    )(z.reshape(bsz, n_chunks, n_pieces, PIECE, nz), *consts, *params)
    return out.reshape(bsz, s, d_out)


def _out_mlp_kernel(mixed_ref, x_ref, gate1_ref, shift_ref, gscale_ref, gate2_ref,
                    wout_ref, w1_ref, w2_ref, fnorm_ref, o_ref, *, final):
    x = x_ref[...] + gate1_ref[...] * jnp.dot(mixed_ref[...], wout_ref[...], preferred_element_type=F32)
    y = _rms(x) * gscale_ref[...] + shift_ref[...]
    u = jnp.dot(y.astype(BF16), w1_ref[...], preferred_element_type=F32)
    u = jnp.square(jnp.maximum(u, 0.0))
    x = x + gate2_ref[...] * jnp.dot(u.astype(BF16), w2_ref[...], preferred_element_type=F32)
    if final:
        x = _rms(x) * fnorm_ref[...]
    o_ref[...] = x


def _out_mlp(mixed, x, gate1, shift, gscale, gate2, w_out, w1, w2, fnorm, *, final, tm=ROW_TILE):
    bsz, s, d = x.shape
    tok = pl.BlockSpec((None, tm, d), lambda b, i: (b, i, 0))
    mod = pl.BlockSpec((None, 1, d), lambda b, i: (b, 0, 0))
    whole = lambda a: pl.BlockSpec(a.shape, lambda b, i: (0,) * a.ndim, pipeline_mode=pl.Buffered(1))
    return pl.pallas_call(
        functools.partial(_out_mlp_kernel, final=final),
        grid=(bsz, s // tm),
        in_specs=[tok, tok, mod, mod, mod, mod, whole(w_out), whole(w1), whole(w2), whole(fnorm)],
        out_specs=tok,
        out_shape=jax.ShapeDtypeStruct((bsz, s, d), F32),
        compiler_params=pltpu.CompilerParams(dimension_semantics=("arbitrary", "arbitrary"),
                                             vmem_limit_bytes=SCOPED_VMEM_BYTES),
        name="out_proj_mlp",
    )(mixed, x, gate1, shift, gscale, gate2, w_out, w1, w2, fnorm)


def kernel(x, c, w_ada, b_ada, norm_mix, norm_mlp, w_in, gla_w_gate, gla_b_gate, gla_norm, hgrn_lb_logits,
           hgrn_norm, mlstm_conv, mlstm_gate_b, mlstm_norm, w_out, w_ff1, w_ff2, final_norm):
    depth, d, _ = w_ada.shape
    bsz = x.shape[0]
    mod = _adaln(c.astype(F32), w_ada.astype(F32), b_ada.astype(F32))
    mod = mod.reshape(depth, bsz, 6, 1, d)
    lower = _lower_bounds(hgrn_lb_logits.astype(F32))
    w_in_p = _permute_in_proj(w_in)
    gate_b = jnp.concatenate([mlstm_gate_b[:, MLSTM_HEADS:], mlstm_gate_b[:, :MLSTM_HEADS],
                              jnp.zeros((depth, LANE - 2 * MLSTM_HEADS), F32)], axis=-1)
    wgate = jnp.zeros((depth, LANE, GLA_QK_PAD), F32)
    wgate = wgate.at[:, GLOW_LANE:GLOW_LANE + GLA_RANK, :GLA_QK].set(gla_w_gate).astype(BF16)
    row = lambda a: a.reshape(depth, 1, -1).astype(F32)
    mixer_params = [wgate, row(jnp.pad(gla_b_gate, ((0, 0), (0, GLA_QK_PAD - GLA_QK)))),
                    row(jnp.concatenate([gla_norm, gla_norm], axis=-1) * HEAD_DIM ** 0.5),
                    row(lower), row(jnp.concatenate([hgrn_norm, hgrn_norm], axis=-1) * HEAD_DIM ** 0.5),
                    mlstm_conv.astype(F32), row(gate_b), row(mlstm_norm * HEAD_DIM ** 0.5)]
    late_weights = [w_out.astype(F32), w_ff1.astype(F32), w_ff2.astype(F32)]
    fnorm = final_norm.reshape(1, d).astype(F32)
    for l in range(depth):
        shift1, scale1, gate1, shift2, scale2, gate2 = (mod[l, :, j] for j in range(6))
        gscale1 = norm_mix[l].astype(F32) * (1.0 + scale1)
        gscale2 = norm_mlp[l].astype(F32) * (1.0 + scale2)
        z, (w_out_b, w1_b, w2_b) = _in_proj(x, shift1, gscale1, w_in_p, l, late_weights)
        mixed = _mixers(z, mixer_params, l)
        x = _out_mlp(mixed, x, gate1, shift2, gscale2, gate2, w_out_b, w1_b, w2_b, fnorm, final=(l == depth - 1))
    return x
```

```python
import functools

import numpy as np
import jax
import jax.numpy as jnp
from jax import lax
from jax.experimental import pallas as pl
from jax.experimental.pallas import tpu as pltpu

F32 = jnp.float32
BF16 = jnp.bfloat16

HEAD_DIM = 64
GLA_HEADS = 6
GLA_DK = 32
GLA_RANK = 16
GLA_TAU = 16.0
HGRN_HEADS = 6
HGRN_DK = 64
MLSTM_HEADS = 4
CONV_WIDTH = 4
EPS = 1e-6
CHUNK = 64
CHUNKS_PER_STEP = 2
PIECE = 16
LANE = 128
NEG = -1e30
MAX_EXP2 = 115.0
LOG2E = 1.4426950408889634
MIB = 1024 * 1024
V7X_VMEM_BYTES = 64 * MIB
SCOPED_VMEM_BYTES = V7X_VMEM_BYTES - 4 * MIB
MIXER_VMEM_BYTES = 48 * MIB
ROW_TILE = 1024
IN_PROJ_ROW_PARTS = 2

GLA_QK = GLA_HEADS * GLA_DK
GLA_W = GLA_HEADS * HEAD_DIM
HGRN_W = HGRN_HEADS * HEAD_DIM
MLSTM_W = MLSTM_HEADS * HEAD_DIM
GLA_QK_PAD = 2 * LANE

_REF_SEGMENTS = (("gq", GLA_QK), ("gk", GLA_QK), ("gv", GLA_W), ("glow", GLA_RANK), ("gout", GLA_W),
                 ("hq", HGRN_W), ("hf", HGRN_W), ("hi", HGRN_W), ("hout", HGRN_W),
                 ("mqk", 2 * MLSTM_W), ("mv", MLSTM_W), ("mi", MLSTM_HEADS), ("mf", MLSTM_HEADS),
                 ("mout", MLSTM_W))
NARROW = 2 * MLSTM_HEADS + GLA_RANK
_MY_ORDER = (("gq", 0), ("mf", 0), ("mi", 0), ("glow", 0), ("pad", GLA_QK_PAD - GLA_QK - NARROW),
             ("gk", 0), ("pad", GLA_QK_PAD - GLA_QK),
             ("gv", 0), ("gout", 0), ("hq", 0), ("hf", 0), ("hi", 0), ("hout", 0),
             ("mqk", 0), ("mv", 0), ("mout", 0))


def _layout():
    ref_off, off = {}, 0
    for name, width in _REF_SEGMENTS:
        ref_off[name] = (off, width)
        off += width
    my_off, off = {}, 0
    for name, pad in _MY_ORDER:
        if name == "pad":
            off += pad
        else:
            my_off[name] = off
            off += ref_off[name][1]
    padded = -(-off // LANE) * LANE
    return ref_off, my_off, off, padded


_REF_OFF, OFF, _NZ_USED, NZ = _layout()
SMALL = OFF["mf"] // LANE * LANE
GATE_LANE = OFF["mf"] - SMALL
GLOW_LANE = OFF["glow"] - SMALL
assert OFF["mi"] == OFF["mf"] + MLSTM_HEADS and GLOW_LANE == GATE_LANE + 2 * MLSTM_HEADS and GLOW_LANE + GLA_RANK <= LANE
assert all(OFF[n] % LANE == 0 for n in ("gq", "gk", "gv", "gout", "hq", "hf", "hi", "hout", "mqk", "mv", "mout"))


def _permute_kernel(wt_ref, o_ref, rows_scr):
    off = 0
    for name, pad in _MY_ORDER:
        if name == "pad":
            rows_scr[off:off + pad, :] = jnp.zeros((pad, rows_scr.shape[1]), F32)
            off += pad
        else:
            lo, width = _REF_OFF[name]
            rows_scr[off:off + width, :] = wt_ref[lo:lo + width, :]
            off += width
    if off < rows_scr.shape[0]:
        rows_scr[off:, :] = jnp.zeros((rows_scr.shape[0] - off, rows_scr.shape[1]), F32)
    for t in range(rows_scr.shape[0] // LANE):
        o_ref[:, t * LANE:(t + 1) * LANE] = rows_scr[t * LANE:(t + 1) * LANE, :].T.astype(o_ref.dtype)


def _permute_in_proj(w_in, *, tc=256):
    depth, d, d_in = w_in.shape
    return pl.pallas_call(
        _permute_kernel,
        grid=(depth, d // tc),
        in_specs=[pl.BlockSpec((None, d_in, tc), lambda l, i: (l, 0, i))],
        out_specs=pl.BlockSpec((None, tc, NZ), lambda l, i: (l, i, 0)),
        out_shape=jax.ShapeDtypeStruct((depth, d, NZ), BF16),
        scratch_shapes=[pltpu.VMEM((NZ, tc), F32)],
        name="permute_in_proj_weight",
    )(jnp.swapaxes(w_in, 1, 2))


def _silu(x):
    return x * jax.nn.sigmoid(x)


def _log2_sigmoid(x):
    return jnp.minimum(x, 0.0) * LOG2E - jnp.log2(1.0 + jnp.exp2(jnp.abs(x) * (-LOG2E)))


def _dot(a, b):
    return jnp.dot(a.astype(BF16), b.astype(BF16), preferred_element_type=F32)


def _dot_nt(a, b):
    return lax.dot_general(a.astype(BF16), b.astype(BF16), (((1,), (1,)), ((), ())),
                           preferred_element_type=F32)


def _cumsum_rows(g, tri):
    hi = g.astype(BF16)
    lo = (g - hi.astype(F32)).astype(BF16)
    return (jnp.dot(tri, hi, preferred_element_type=F32) + jnp.dot(tri, lo, preferred_element_type=F32))


def _adaln_kernel(c_ref, w_ref, b_ref, o_ref):
    cond = _silu(c_ref[...])
    o_ref[...] = _dot(cond, w_ref[...]) + b_ref[...]


def _adaln(c, w_ada, b_ada, *, tn=2048):
    depth, d, n = w_ada.shape
    bsz = c.shape[0]
    return pl.pallas_call(
        _adaln_kernel,
        grid=(depth, n // tn),
        in_specs=[pl.BlockSpec((bsz, d), lambda l, j: (0, 0)),
                  pl.BlockSpec((None, d, tn), lambda l, j: (l, 0, j)),
                  pl.BlockSpec((None, 1, tn), lambda l, j: (l, 0, j))],
        out_specs=pl.BlockSpec((None, bsz, tn), lambda l, j: (l, 0, j)),
        out_shape=jax.ShapeDtypeStruct((depth, bsz, n), F32),
        name="adaln_modulation",
    )(c, w_ada, b_ada.reshape(depth, 1, n))


def _lower_bound_kernel(logit_ref, o_ref):
    x = logit_ref[...]
    e = jnp.exp(x - jnp.max(x, axis=0, keepdims=True))
    p = e / jnp.sum(e, axis=0, keepdims=True)
    depth = x.shape[0]
    acc = p[0:1]
    rows = [acc]
    for l in range(1, depth):
        acc = acc + p[l:l + 1]
        rows.append(acc)
    cum = jnp.concatenate(rows, axis=0)
    o_ref[...] = cum - cum[0:1]


def _lower_bounds(lb_logits):
    return pl.pallas_call(
        _lower_bound_kernel,
        out_shape=jax.ShapeDtypeStruct(lb_logits.shape, F32),
        name="hgrn_lower_bounds",
    )(lb_logits)


def _rms(x):
    return x * lax.rsqrt(jnp.mean(x * x, axis=-1, keepdims=True) + EPS)


def _layer_block(arr, layer):
    tail = arr.shape[1:]
    return pl.BlockSpec((None,) + tail, lambda *_: (layer,) + (0,) * len(tail), pipeline_mode=pl.Buffered(1))


def _in_proj_kernel(x_ref, shift_ref, gscale_ref, w_ref, *rest):
    n_cast = (len(rest) - 1) // 2
    z_ref = rest[n_cast]
    part = x_ref.shape[0] // IN_PROJ_ROW_PARTS
    ys = []
    for h in range(IN_PROJ_ROW_PARTS):
        ys.append((_rms(x_ref[h * part:(h + 1) * part, :]) * gscale_ref[...] + shift_ref[...]).astype(BF16))
    for h in range(IN_PROJ_ROW_PARTS):
        z_ref[h * part:(h + 1) * part, :] = jnp.dot(ys[h], w_ref[...], preferred_element_type=F32)
    for src, dst in zip(rest[:n_cast], rest[n_cast + 1:]):
        dst[...] = src[...].astype(dst.dtype)


def _in_proj(x, shift, gscale, w, layer, cast, *, tm=ROW_TILE):
    bsz, s, d = x.shape
    nz = w.shape[-1]
    n_i = s // tm
    steps = bsz * n_i
    slab = lambda a: a.shape[1] // steps
    outs = pl.pallas_call(
        _in_proj_kernel,
        grid=(bsz, n_i),
        in_specs=([pl.BlockSpec((None, tm, d), lambda b, i: (b, i, 0)),
                   pl.BlockSpec((None, 1, d), lambda b, i: (b, 0, 0)),
                   pl.BlockSpec((None, 1, d), lambda b, i: (b, 0, 0)),
                   _layer_block(w, layer)]
                  + [pl.BlockSpec((None, slab(a), a.shape[2]), lambda b, i: (layer, b * n_i + i, 0)) for a in cast]),
        out_specs=([pl.BlockSpec((None, tm, nz), lambda b, i: (b, i, 0))]
                   + [pl.BlockSpec((slab(a), a.shape[2]), lambda b, i: (b * n_i + i, 0)) for a in cast]),
        out_shape=([jax.ShapeDtypeStruct((bsz, s, nz), F32)]
                   + [jax.ShapeDtypeStruct(a.shape[1:], BF16) for a in cast]),
        compiler_params=pltpu.CompilerParams(dimension_semantics=("arbitrary", "arbitrary"),
                                             vmem_limit_bytes=SCOPED_VMEM_BYTES),
        name="norm_in_proj",
    )(x, shift, gscale, w, *cast)
    return outs[0], outs[1:]


def _stack_masks(bsz):
    r = np.arange(bsz * CHUNK)
    batch = (r // PIECE) % bsz
    pos = (r // (PIECE * bsz)) * PIECE + r % PIECE
    return ((batch[:, None] == batch[None, :]) & (pos[None, :] <= pos[:, None])).astype(np.float32)


def _shift_matrix(bsz):
    rows = bsz * CHUNK
    r = np.arange(rows)
    batch = (r // PIECE) % bsz
    pos = (r // (PIECE * bsz)) * PIECE + r % PIECE
    same = batch[:, None] == batch[None, :]
    blocks = []
    for d in range(CONV_WIDTH - 1, 0, -1):
        cur = same & (pos[None, :] == pos[:, None] - d)
        prev = same & (pos[None, :] == pos[:, None] - d + CHUNK)
        blocks.append(np.concatenate([cur, prev], axis=1))
    return np.concatenate(blocks, axis=0).astype(np.float32)


def _pieces(x, bsz):
    n_groups = x.shape[0] // (PIECE * bsz)
    return [[x[(i * bsz + b) * PIECE:(i * bsz + b + 1) * PIECE] for b in range(bsz)] for i in range(n_groups)]


def _per_batch_rows(rows, bsz, reps):
    slab = jnp.concatenate([jnp.broadcast_to(rows[b], (PIECE, rows[b].shape[-1])) for b in range(bsz)], axis=0)
    return jnp.concatenate([slab] * reps, axis=0) if reps > 1 else slab


def _per_batch_select(vals, batch_ids, bsz):
    out = vals[0]
    for b in range(1, bsz):
        out = jnp.where(batch_ids == b, vals[b], out)
    return out


def _halves(lane, even, odd):
    return jnp.where(lane < HEAD_DIM, even, odd)


def _head_sums(x, lane):
    lo = jnp.sum(jnp.where(lane < HEAD_DIM, x, 0.0), axis=-1, keepdims=True)
    hi = jnp.sum(jnp.where(lane < HEAD_DIM, 0.0, x), axis=-1, keepdims=True)
    return _halves(lane, lo, hi)


class _Carry:
    def __init__(self, refs, shared, last):
        self.refs, self.shared, self.last = refs, shared, last
        self.out = {}

    def read(self, name, idx):
        if self.shared is None:
            return self.refs[name][idx]
        while (name, idx) not in self.shared:
            yield
        return self.shared[(name, idx)]

    def write(self, name, idx, val):
        if self.last:
            self.refs[name][idx] = val
        else:
            self.out[(name, idx)] = val


def _decayed_attention(q, k, g, v_tile, carry, name, tri, bsz, *, heads, dk, n_groups):
    rows, wk = q.shape
    glen = CHUNK // n_groups
    gp = glen // PIECE
    grows = rows // n_groups
    b = _cumsum_rows(g, tri)
    bp = _pieces(b, bsz)

    def ref_rows(gi):
        mid = gi * glen + glen // 2
        return _per_batch_rows([bp[mid // PIECE][bb][mid % PIECE:mid % PIECE + 1] for bb in range(bsz)], bsz, gp)

    refs = [ref_rows(gi) for gi in range(n_groups)]
    q_in = q * jnp.exp2(jnp.minimum(b - jnp.concatenate(refs, axis=0), MAX_EXP2))
    q_state = q * jnp.exp2(b)
    last = [bp[-1][bb][PIECE - 1:PIECE] for bb in range(bsz)]
    k_state = (k * jnp.exp2(_per_batch_rows(last, bsz, rows // (bsz * PIECE)) - b)).astype(BF16)
    yield
    k_in = []
    for gi in range(n_groups):
        lo, end = gi * grows, (gi + 1) * grows
        own = k[lo:end] * jnp.exp2(jnp.minimum(refs[gi] - b[lo:end], MAX_EXP2))
        if gi:
            ref = jnp.concatenate([refs[gi]] * gi, axis=0) if gi > 1 else refs[gi]
            own = jnp.concatenate([k[:lo] * jnp.exp2(ref - b[:lo]), own], axis=0)
        k_in.append(own.astype(BF16))
    yield

    lane = lax.broadcasted_iota(jnp.int32, (1, LANE), 1)
    heads_per_tile = LANE // dk
    n_pairs = heads // 2
    probs = [None] * heads
    for kt in range(-(-heads // heads_per_tile)):
        tile_heads = list(range(kt * heads_per_tile, min(heads, (kt + 1) * heads_per_tile)))
        tl = slice(kt * LANE, (kt + 1) * LANE)
        per_head = {h: [] for h in tile_heads}
        for gi in range(n_groups):
            end = (gi + 1) * grows
            qg = q_in[gi * grows:end, tl]
            lhs = jnp.concatenate(
                [jnp.where((lane >= (h % heads_per_tile) * dk) & (lane < (h % heads_per_tile + 1) * dk), qg, 0.0)
                 for h in tile_heads], axis=0).astype(BF16)
            sc = _dot_nt(lhs, k_in[gi][:, tl])
            keep = tri[gi * grows:end, :end] > 0
            for n, h in enumerate(tile_heads):
                blk = sc[n * grows:(n + 1) * grows].astype(BF16)
                blk = jnp.where(keep, blk, jnp.zeros_like(blk))
                if end < rows:
                    blk = jnp.concatenate([blk, jnp.zeros((grows, rows - end), BF16)], axis=1)
                per_head[h].append(blk)
        for h in tile_heads:
            probs[h] = jnp.concatenate(per_head[h], axis=0)
        yield

    lane_batch = (lax.broadcasted_iota(jnp.int32, (1, rows), 1) // PIECE) % bsz
    row_i = lax.broadcasted_iota(jnp.int32, (LANE, LANE), 0)
    col_i = lax.broadcasted_iota(jnp.int32, (LANE, LANE), 1)
    outs = []
    for p in range(n_pairs):
        kt = (2 * p) // heads_per_tile
        tl = slice(kt * LANE, (kt + 1) * LANE)
        v = v_tile(p)
        vb = v.astype(BF16)
        intra = _halves(lane, _dot(probs[2 * p], vb), _dot(probs[2 * p + 1], vb))
        qs = _pieces(q_state[:, tl], bsz)
        inter_b, states = [], []
        for bb in range(bsz):
            lhs = jnp.concatenate([qs[i][bb] for i in range(len(qs))], axis=0)
            states.append((yield from carry.read(name, (p, bb))))
            inter_b.append(_dot_nt(lhs, states[bb]))
        inter = jnp.concatenate([inter_b[bb][i * PIECE:(i + 1) * PIECE]
                                 for i in range(len(qs)) for bb in range(bsz)], axis=0)
        outs.append(intra + inter)
        yield
        vt = v.T.astype(BF16)
        lhs = jnp.concatenate([jnp.where(lane_batch == bb, vt, jnp.zeros_like(vt)) for bb in range(bsz)], axis=0)
        upd = _dot(lhs, k_state[:, tl])
        same_head = (2 * p + row_i // HEAD_DIM) == (kt * heads_per_tile + col_i // dk)
        for bb in range(bsz):
            decay = jnp.exp2(last[bb][:, tl])
            carry.write(name, (p, bb),
                        states[bb] * decay + jnp.where(same_head, upd[bb * LANE:(bb + 1) * LANE], 0.0))
        yield
    return outs


def _mixer_kernel(z_ref, negmask_ref, tri_ref, shift_ref, wgate_ref, bgate_ref, gnorm_ref, lb_ref, hnorm_ref,
                  conv_ref, gateb_ref, mnorm_ref, o_ref, gla_state, hgrn_state, mlstm_state, mlstm_m, conv_carry,
                  *, bsz):
    @pl.when(pl.program_id(0) == 0)
    def _():
        gla_state[...] = jnp.zeros_like(gla_state)
        hgrn_state[...] = jnp.zeros_like(hgrn_state)
        mlstm_state[...] = jnp.zeros_like(mlstm_state)
        mlstm_m[...] = jnp.zeros_like(mlstm_m)
        conv_carry[...] = jnp.zeros_like(conv_carry)

    n_pieces = CHUNK // PIECE
    rows = bsz * CHUNK
    tri = tri_ref[...]
    lane = lax.broadcasted_iota(jnp.int32, (1, LANE), 1)
    refs = {"gla": gla_state, "hgrn": hgrn_state, "mlstm": mlstm_state, "m": mlstm_m, "conv": conv_carry}
    everything = (Ellipsis,)

    pending, shared = [], None
    for ci in range(CHUNKS_PER_STEP):
        carry = _Carry(refs, shared, last=(ci == CHUNKS_PER_STEP - 1))
        pending += _chunk_mixers(ci, carry, z_ref, o_ref, negmask_ref, tri, shift_ref, wgate_ref, bgate_ref,
                                 gnorm_ref, lb_ref, hnorm_ref, conv_ref, gateb_ref, mnorm_ref, lane, everything,
                                 bsz=bsz, n_pieces=n_pieces, rows=rows)
        shared = carry.out
    while pending:
        for gen in list(pending):
            try:
                next(gen)
            except StopIteration:
                pending.remove(gen)


def _chunk_mixers(ci, carry, z_ref, o_ref, negmask_ref, tri, shift_ref, wgate_ref, bgate_ref, gnorm_ref,
                  lb_ref, hnorm_ref, conv_ref, gateb_ref, mnorm_ref, lane, everything, *, bsz, n_pieces, rows):
    def load(off, width):
        return jnp.concatenate([z_ref[b, ci, i, :, off:off + width] for i in range(n_pieces) for b in range(bsz)],
                               axis=0)

    def store(tile_idx, val):
        val = val.astype(o_ref.dtype)
        for i in range(n_pieces):
            for b in range(bsz):
                r0 = (i * bsz + b) * PIECE
                o_ref[b, ci, i, :, tile_idx * LANE:(tile_idx + 1) * LANE] = val[r0:r0 + PIECE]

    small = load(SMALL, LANE)

    def gla_group():
        logits = _dot(small, wgate_ref[...]) + bgate_ref[...]
        g = _log2_sigmoid(logits) * (1.0 / GLA_TAU)
        q = load(OFF["gq"], GLA_QK_PAD) * (GLA_DK ** -0.5)
        k = load(OFF["gk"], GLA_QK_PAD)
        outs = yield from _decayed_attention(q, k, g, lambda p: load(OFF["gv"] + p * LANE, LANE), carry, "gla",
                                             tri, bsz, heads=GLA_HEADS, dk=GLA_DK, n_groups=4)
        for p in range(GLA_HEADS // 2):
            o = outs[p]
            o = o * lax.rsqrt(_head_sums(o * o, lane) + HEAD_DIM * EPS) * gnorm_ref[...]
            o = o * _silu(load(OFF["gout"] + p * LANE, LANE))
            store(p, o)
            yield

    def hgrn_group():
        lb = lb_ref[...]
        f = load(OFF["hf"], HGRN_W)
        t = jnp.exp2(jnp.abs(f) * (-LOG2E))
        r = 1.0 / (1.0 + t)
        pos = f > 0.0
        fg = lb + (1.0 - lb) * (jnp.where(pos, 1.0, t) * r)
        g = jnp.maximum(jnp.log2(fg), jnp.log2(1.0 - lb) + (jnp.minimum(f, 0.0) * LOG2E - 1.0))
        k = (1.0 - lb) * (jnp.where(pos, t, 1.0) * r)
        q = _silu(load(OFF["hq"], HGRN_W))
        yield
        outs = yield from _decayed_attention(q, k, g, lambda p: load(OFF["hi"] + p * LANE, LANE), carry, "hgrn",
                                             tri, bsz, heads=HGRN_HEADS, dk=HGRN_DK, n_groups=4)
        for p in range(HGRN_HEADS // 2):
            o = outs[p]
            o = o * lax.rsqrt(_head_sums(o * o, lane) + HEAD_DIM * EPS) * hnorm_ref[...]
            o = o * jax.nn.sigmoid(load(OFF["hout"] + p * LANE, LANE))
            store(GLA_W // LANE + p, o)
            yield

    def mlstm_group():
        x_qk = load(OFF["mqk"], 2 * MLSTM_W)
        x_bf = x_qk.astype(BF16)
        x_prev = yield from carry.read("conv", everything)
        delayed = jnp.dot(shift_ref[...], jnp.concatenate([x_bf, x_prev], axis=0),
                          preferred_element_type=F32)
        carry.write("conv", everything, x_bf)
        yield
        acc = conv_ref[CONV_WIDTH - 1:CONV_WIDTH, :] * x_qk
        for tap in range(CONV_WIDTH - 1):
            acc = acc + conv_ref[tap:tap + 1, :] * delayed[tap * rows:(tap + 1) * rows]
        qk = _silu(acc)
        yield
        pre = small + gateb_ref[...]
        is_f = (lane >= GATE_LANE) & (lane < GATE_LANE + MLSTM_HEADS)
        is_i = (lane >= GATE_LANE + MLSTM_HEADS) & (lane < GATE_LANE + 2 * MLSTM_HEADS)
        gates = jnp.where(is_f, _log2_sigmoid(pre), jnp.where(is_i, pre * LOG2E, 0.0))
        cum = _cumsum_rows(gates, tri)
        row_form = jnp.where(is_f, cum, gates).T
        lane_batch = (lax.broadcasted_iota(jnp.int32, (1, rows), 1) // PIECE) % bsz
        row_batch = (lax.broadcasted_iota(jnp.int32, (rows, 1), 0) // PIECE) % bsz
        ones_tile = jnp.ones((rows, LANE), BF16)
        yield
        for p in range(MLSTM_HEADS // 2):
            q_pair = qk[:, p * LANE:(p + 1) * LANE]
            k_pair = qk[:, MLSTM_W + p * LANE:MLSTM_W + (p + 1) * LANE] * (HEAD_DIM ** -0.5)
            kt_pair = k_pair.T
            v2 = jnp.concatenate([load(OFF["mv"] + p * LANE, LANE).astype(BF16), ones_tile], axis=1)
            y_heads = []
            for hh in range(2):
                h = 2 * p + hh
                own = (lane >= hh * HEAD_DIM) & (lane < (hh + 1) * HEAD_DIM)
                b_row = row_form[GATE_LANE + h:GATE_LANE + h + 1, :]
                c_row = row_form[GATE_LANE + MLSTM_HEADS + h:GATE_LANE + MLSTM_HEADS + h + 1, :] - b_row
                b_col = jnp.sum(jnp.where(lane == GATE_LANE + h, cum, 0.0), axis=-1, keepdims=True)
                m_prev, c_prev = [], []
                for bb in range(bsz):
                    m_prev.append((yield from carry.read("m", (slice(bb, bb + 1), slice(h, h + 1)))))
                    c_prev.append((yield from carry.read("mlstm", (h, bb))))
                m_prev_col = _per_batch_select(m_prev, row_batch, bsz)
                c_masked = c_row + negmask_ref[...]
                mu = jnp.maximum(m_prev_col, jnp.max(c_masked, axis=-1, keepdims=True))
                q_own = jnp.where(own, q_pair, 0.0)
                prob = jnp.exp2(c_masked - mu) * _dot_nt(q_own, k_pair)
                res = _dot(prob, v2)
                qo = _pieces(q_own, bsz)
                inter_b = []
                for bb in range(bsz):
                    lhs = jnp.concatenate([qo[i][bb] for i in range(n_pieces)], axis=0)
                    inter_b.append(_dot(lhs, c_prev[bb]))
                inter = jnp.concatenate([inter_b[bb][i * PIECE:(i + 1) * PIECE]
                                         for i in range(n_pieces) for bb in range(bsz)], axis=0)
                res = res + jnp.exp2(m_prev_col - mu) * inter
                den = jnp.maximum(jnp.abs(res[:, LANE:]), jnp.exp2(-(b_col + mu)))
                y_heads.append(res[:, :LANE] / den)
                yield
                b_end = [b_col[rows - (bsz - bb) * PIECE + PIECE - 1:rows - (bsz - bb) * PIECE + PIECE, :]
                         for bb in range(bsz)]
                w_row = _per_batch_select(b_end, lane_batch, bsz) + c_row
                m_loc = [jnp.max(jnp.where(lane_batch == bb, w_row, NEG), axis=-1, keepdims=True)
                         for bb in range(bsz)]
                m_new = [jnp.maximum(b_end[bb] + m_prev[bb], m_loc[bb]) for bb in range(bsz)]
                ks_t = (kt_pair * jnp.exp2(w_row - _per_batch_select(m_new, lane_batch, bsz))).astype(BF16)
                lhs = jnp.concatenate([jnp.where(lane_batch == bb, ks_t, jnp.zeros_like(ks_t)) for bb in range(bsz)],
                                      axis=0)
                upd = _dot(lhs, v2)
                for bb in range(bsz):
                    s_old = jnp.exp2(b_end[bb] + m_prev[bb] - m_new[bb])
                    carry.write("mlstm", (h, bb), s_old * c_prev[bb] + upd[bb * LANE:(bb + 1) * LANE])
                    carry.write("m", (slice(bb, bb + 1), slice(h, h + 1)), m_new[bb])
                yield
            y = _halves(lane, y_heads[0], y_heads[1])
            mu_h = _head_sums(y, lane) * (1.0 / HEAD_DIM)
            yc = y - mu_h
            sq = _head_sums(yc * yc, lane)
            y = yc * lax.rsqrt(sq + HEAD_DIM * EPS) * mnorm_ref[:, p * LANE:(p + 1) * LANE]
            y = y * jax.nn.sigmoid(load(OFF["mout"] + p * LANE, LANE))
            store((GLA_W + HGRN_W) // LANE + p, y)
            yield

    return [gla_group(), hgrn_group(), mlstm_group()]


def _mixers(z, params, layer):
    bsz, s, nz = z.shape
    d_out = GLA_W + HGRN_W + MLSTM_W
    n_chunks = s // CHUNK
    n_pieces = CHUNK // PIECE
    rows = bsz * CHUNK
    mask = _stack_masks(bsz)
    consts = [jnp.asarray((1.0 - mask) * NEG), jnp.asarray(mask, dtype=BF16),
              jnp.asarray(_shift_matrix(bsz), dtype=BF16)]
    whole = lambda a: pl.BlockSpec(a.shape, lambda c: (0,) * a.ndim, pipeline_mode=pl.Buffered(1))
    out = pl.pallas_call(
        functools.partial(_mixer_kernel, bsz=bsz),
        grid=(n_chunks // CHUNKS_PER_STEP,),
        in_specs=([pl.BlockSpec((bsz, CHUNKS_PER_STEP, n_pieces, PIECE, nz), lambda c: (0, c, 0, 0, 0))]
                  + [whole(a) for a in consts] + [_layer_block(a, layer) for a in params]),
        out_specs=pl.BlockSpec((bsz, CHUNKS_PER_STEP, n_pieces, PIECE, d_out), lambda c: (0, c, 0, 0, 0)),
        out_shape=jax.ShapeDtypeStruct((bsz, n_chunks, n_pieces, PIECE, d_out), BF16),
        scratch_shapes=[pltpu.VMEM((GLA_HEADS // 2, bsz, LANE, LANE), F32),
                        pltpu.VMEM((HGRN_HEADS // 2, bsz, LANE, LANE), F32),
                        pltpu.VMEM((MLSTM_HEADS, bsz, LANE, 2 * LANE), F32),
                        pltpu.VMEM((8, LANE), F32),
                        pltpu.VMEM((rows, 2 * MLSTM_W), BF16)],
        compiler_params=pltpu.CompilerParams(dimension_semantics=("arbitrary",),
                                             vmem_limit_bytes=MIXER_VMEM_BYTES),
        name="token_mixers",
    )(z.reshape(bsz, n_chunks, n_pieces, PIECE, nz), *consts, *params)
    return out.reshape(bsz, s, d_out)


def _out_mlp_kernel(mixed_ref, x_ref, gate1_ref, shift_ref, gscale_ref, gate2_ref,
                    wout_ref, w1_ref, w2_ref, fnorm_ref, o_ref, *, final):
    x = x_ref[...] + gate1_ref[...] * jnp.dot(mixed_ref[...], wout_ref[...], preferred_element_type=F32)
    y = _rms(x) * gscale_ref[...] + shift_ref[...]
    u = jnp.dot(y.astype(BF16), w1_ref[...], preferred_element_type=F32)
    u = jnp.square(jnp.maximum(u, 0.0))
    x = x + gate2_ref[...] * jnp.dot(u.astype(BF16), w2_ref[...], preferred_element_type=F32)
    if final:
        x = _rms(x) * fnorm_ref[...]
    o_ref[...] = x


def _out_mlp(mixed, x, gate1, shift, gscale, gate2, w_out, w1, w2, fnorm, *, final, tm=ROW_TILE):
    bsz, s, d = x.shape
    tok = pl.BlockSpec((None, tm, d), lambda b, i: (b, i, 0))
    mod = pl.BlockSpec((None, 1, d), lambda b, i: (b, 0, 0))
    whole = lambda a: pl.BlockSpec(a.shape, lambda b, i: (0,) * a.ndim, pipeline_mode=pl.Buffered(1))
    return pl.pallas_call(
        functools.partial(_out_mlp_kernel, final=final),
        grid=(bsz, s // tm),
        in_specs=[tok, tok, mod, mod, mod, mod, whole(w_out), whole(w1), whole(w2), whole(fnorm)],
        out_specs=tok,
        out_shape=jax.ShapeDtypeStruct((bsz, s, d), F32),
        compiler_params=pltpu.CompilerParams(dimension_semantics=("arbitrary", "arbitrary"),
                                             vmem_limit_bytes=SCOPED_VMEM_BYTES),
        name="out_proj_mlp",
    )(mixed, x, gate1, shift, gscale, gate2, w_out, w1, w2, fnorm)


def kernel(x, c, w_ada, b_ada, norm_mix, norm_mlp, w_in, gla_w_gate, gla_b_gate, gla_norm, hgrn_lb_logits,
           hgrn_norm, mlstm_conv, mlstm_gate_b, mlstm_norm, w_out, w_ff1, w_ff2, final_norm):
    depth, d, _ = w_ada.shape
    bsz = x.shape[0]
    mod = _adaln(c.astype(F32), w_ada.astype(F32), b_ada.astype(F32))
    mod = mod.reshape(depth, bsz, 6, 1, d)
    lower = _lower_bounds(hgrn_lb_logits.astype(F32))
    w_in_p = _permute_in_proj(w_in)
    gate_b = jnp.concatenate([jnp.zeros((depth, GATE_LANE), F32),
                              mlstm_gate_b[:, MLSTM_HEADS:], mlstm_gate_b[:, :MLSTM_HEADS],
                              jnp.zeros((depth, LANE - GATE_LANE - 2 * MLSTM_HEADS), F32)], axis=-1)
    wgate = jnp.zeros((depth, LANE, GLA_QK_PAD), F32)
    wgate = wgate.at[:, GLOW_LANE:GLOW_LANE + GLA_RANK, :GLA_QK].set(gla_w_gate).astype(BF16)
    row = lambda a: a.reshape(depth, 1, -1).astype(F32)
    mixer_params = [wgate, row(jnp.pad(gla_b_gate, ((0, 0), (0, GLA_QK_PAD - GLA_QK)))),
                    row(jnp.concatenate([gla_norm, gla_norm], axis=-1) * HEAD_DIM ** 0.5),
                    row(lower), row(jnp.concatenate([hgrn_norm, hgrn_norm], axis=-1) * HEAD_DIM ** 0.5),
                    mlstm_conv.astype(F32), row(gate_b), row(mlstm_norm * HEAD_DIM ** 0.5)]
    late_weights = [w_out.astype(F32), w_ff1.astype(F32), w_ff2.astype(F32)]
    fnorm = final_norm.reshape(1, d).astype(F32)
    for l in range(depth):
        shift1, scale1, gate1, shift2, scale2, gate2 = (mod[l, :, j] for j in range(6))
        gscale1 = norm_mix[l].astype(F32) * (1.0 + scale1)
        gscale2 = norm_mlp[l].astype(F32) * (1.0 + scale2)
        z, (w_out_b, w1_b, w2_b) = _in_proj(x, shift1, gscale1, w_in_p, l, late_weights)
        mixed = _mixers(z, mixer_params, l)
        x = _out_mlp(mixed, x, gate1, shift2, gscale2, gate2, w_out_b, w1_b, w2_b, fnorm, final=(l == depth - 1))
    return x
```
